```python
import jax, jax.numpy as jnp
from jax import lax
import numpy as np

D_MODEL = 1024
BATCH = 8
SEQ = 4096
DEPTH = 4

N_MEM = 256
HEAD_DIM = 64
EPS = 1e-6
A_HEADS = 8
A_WIDTH = A_HEADS * HEAD_DIM
CHUNK = 128
POOL_WINDOWS = (2, 4, 8, 16)
B_GROUPS = len(POOL_WINDOWS)
B_GROUP_DIM = 128
B_WIDTH = B_GROUPS * B_GROUP_DIM
C_Q_HEADS = 8
C_KV_HEADS = 2
C_GROUP = C_Q_HEADS // C_KV_HEADS
C_Q_WIDTH = C_Q_HEADS * HEAD_DIM
C_KV_WIDTH = C_KV_HEADS * HEAD_DIM
WINDOW = 128
BLOCK = 128
D_WIDTH = 512
CONV_WIDTH = 31
M_HEADS = 4
M_WIDTH = M_HEADS * HEAD_DIM
OUT_WIDTH = A_WIDTH + B_WIDTH + M_WIDTH
EVEN_SIZES = (A_WIDTH, A_WIDTH, A_WIDTH, B_WIDTH, B_WIDTH, M_WIDTH, M_WIDTH)
ODD_SIZES = (C_Q_WIDTH, C_KV_WIDTH, C_KV_WIDTH, C_Q_WIDTH, D_WIDTH, D_WIDTH, D_WIDTH, M_WIDTH, M_WIDTH)
EVEN_IN = sum(EVEN_SIZES)
ODD_IN = sum(ODD_SIZES)
N_EVEN = (DEPTH + 1) // 2
N_ODD = DEPTH // 2

kernel_name = "hybrid_gmlp_pool_swa_conv_memory_trunk"


def _split(z, sizes):
    idx = list(np.cumsum(sizes)[:-1])
    return jnp.split(z, idx, axis=-1)


def rms_norm(x, g):
    xf = x.astype(jnp.float32)
    y = xf * lax.rsqrt(jnp.mean(xf * xf, axis=-1, keepdims=True) + EPS)
    return (y * g.astype(jnp.float32)).astype(x.dtype)


def layer_norm(x, g, b):
    xf = x.astype(jnp.float32)
    mu = jnp.mean(xf, axis=-1, keepdims=True)
    xc = xf - mu
    y = xc * lax.rsqrt(jnp.mean(xc * xc, axis=-1, keepdims=True) + EPS)
    return (y * g.astype(jnp.float32) + b.astype(jnp.float32)).astype(x.dtype)


def chunked_sgu(u, v, ln_g, ln_b, w_s, b_s):
    bn, s, _ = u.shape
    nc = s // CHUNK
    vn = layer_norm(v, ln_g, ln_b).reshape(bn, nc, CHUNK, A_HEADS, HEAD_DIM)
    mask = jnp.tril(jnp.ones((CHUNK, CHUNK), dtype=bool))[None]
    w = jnp.where(mask, w_s, jnp.zeros_like(w_s)).astype(vn.dtype)
    sg = jnp.einsum('hts,bnshd->bnthd', w, vn) + b_s.T.astype(vn.dtype)[None, None, :, :, None]
    return u * sg.reshape(bn, s, A_WIDTH)


def multiscale_pool(xb, w_g, scale):
    bn, s, _ = xb.shape
    xg = xb.reshape(bn, s, B_GROUPS, B_GROUP_DIM)
    xf = xg.astype(jnp.float32)
    c = jnp.cumsum(xf, axis=1)
    t = jnp.arange(s)
    outs = []
    for g, w in enumerate(POOL_WINDOWS):
        cg = c[:, :, g]
        prev = jnp.pad(cg, ((0, 0), (w, 0), (0, 0)))[:, :s]
        cnt = jnp.minimum(t + 1, w).astype(jnp.float32)[None, :, None]
        outs.append((cg - prev) / cnt)
    pooled = jnp.stack(outs, axis=2)
    diff = (pooled - xf).astype(xb.dtype)
    y = jnp.einsum('bsgc,gcd->bsgd', diff, w_g)
    return y.reshape(bn, s, B_WIDTH) * scale


def swa_sink_attention(q, k, v, qn, kn, sink):
    bn, s, _ = q.shape
    nb = s // BLOCK
    q = rms_norm(q.reshape(bn, s, C_Q_HEADS, HEAD_DIM), qn)
    k = rms_norm(k.reshape(bn, s, C_KV_HEADS, HEAD_DIM), kn)
    v = v.reshape(bn, s, C_KV_HEADS, HEAD_DIM)
    qb = q.reshape(bn, nb, BLOCK, C_KV_HEADS, C_GROUP, HEAD_DIM)

    def band(z):
        zb = z.reshape(bn, nb, BLOCK, C_KV_HEADS, HEAD_DIM)
        prev = jnp.pad(zb, ((0, 0), (1, 0), (0, 0), (0, 0), (0, 0)))[:, :nb]
        return jnp.concatenate([prev, zb], axis=2)

    kb, vb = band(k), band(v)
    sc = jnp.einsum('bnqkgd,bnskd->bnkgqs', qb, kb).astype(jnp.float32) * (HEAD_DIM ** -0.5)
    qi = jnp.arange(BLOCK)[:, None]
    kj = jnp.arange(2 * BLOCK)[None, :]
    rel = qi + BLOCK - kj
    key_abs = jnp.arange(nb)[:, None] * BLOCK + jnp.arange(2 * BLOCK)[None, :] - BLOCK
    valid = ((rel >= 0) & (rel < WINDOW))[None] & (key_abs >= 0)[:, None, :]
    sc = jnp.where(valid[None, :, None, None], sc, jnp.finfo(jnp.float32).min)
    sk = sink.astype(jnp.float32).reshape(C_KV_HEADS, C_GROUP)[None, None, :, :, None]
    m = jnp.maximum(jnp.max(sc, axis=-1), sk)
    p = jnp.exp(sc - m[..., None])
    denom = jnp.sum(p, axis=-1) + jnp.exp(sk - m)
    p = (p / denom[..., None]).astype(vb.dtype)
    o = jnp.einsum('bnkgqs,bnskd->bnqkgd', p, vb)
    return o.reshape(bn, s, C_Q_WIDTH)


def conformer_conv(a, b, dw, dw_b, ln_g, ln_b, pw):
    h = a * jax.nn.sigmoid(b)
    h = lax.conv_general_dilated(h, dw[:, None, :].astype(h.dtype), window_strides=(1,),
                                 padding=[(CONV_WIDTH - 1, 0)],
                                 dimension_numbers=('NWC', 'WIO', 'NWC'),
                                 feature_group_count=D_WIDTH) + dw_b
    h = jax.nn.silu(layer_norm(h, ln_g, ln_b))
    return h @ pw


def memory_attention(q, mem_n, w_kv, qn, kn):
    bn, s, _ = q.shape
    q = rms_norm(q.reshape(bn, s, M_HEADS, HEAD_DIM), qn)
    k, v = jnp.split(mem_n @ w_kv, 2, axis=-1)
    k = rms_norm(k.reshape(bn, N_MEM, M_HEADS, HEAD_DIM), kn)
    v = v.reshape(bn, N_MEM, M_HEADS, HEAD_DIM)
    sc = jnp.einsum('bshd,bmhd->bhsm', q, k).astype(jnp.float32) * (HEAD_DIM ** -0.5)
    p = jax.nn.softmax(sc, axis=-1).astype(v.dtype)
    o = jnp.einsum('bhsm,bmhd->bshd', p, v)
    return o.reshape(bn, s, M_WIDTH)


def setup_inputs(seed: int = 0) -> dict:
    key = jax.random.key(seed)
    ks = jax.random.split(key, 32)
    nrm = lambda k, shape, s: jax.random.normal(k, shape, jnp.float32) * s
    gain = lambda k, shape: 1.0 + 0.02 * jax.random.normal(k, shape, jnp.float32)
    return {
        "x": nrm(ks[0], (BATCH, SEQ, D_MODEL), 1.0),
        "mem": nrm(ks[1], (BATCH, N_MEM, D_MODEL), 1.0),
        "norm_g": gain(ks[2], (DEPTH, D_MODEL)),
        "mem_norm_g": gain(ks[3], (D_MODEL,)),
        "w_mem_kv": nrm(ks[4], (DEPTH, D_MODEL, 2 * M_WIDTH), D_MODEL ** -0.5),
        "m_qnorm": gain(ks[5], (DEPTH, HEAD_DIM)),
        "m_knorm": gain(ks[6], (DEPTH, HEAD_DIM)),
        "w_in_even": nrm(ks[7], (N_EVEN, D_MODEL, EVEN_IN), D_MODEL ** -0.5),
        "w_out_even": nrm(ks[8], (N_EVEN, OUT_WIDTH, D_MODEL), 0.5 * OUT_WIDTH ** -0.5),
        "a_ln_g": gain(ks[9], (N_EVEN, A_WIDTH)),
        "a_ln_b": nrm(ks[10], (N_EVEN, A_WIDTH), 0.02),
        "a_ws": nrm(ks[11], (N_EVEN, A_HEADS, CHUNK, CHUNK), CHUNK ** -0.5),
        "a_bs": gain(ks[12], (N_EVEN, A_HEADS, CHUNK)),
        "b_w": nrm(ks[13], (N_EVEN, B_GROUPS, B_GROUP_DIM, B_GROUP_DIM), B_GROUP_DIM ** -0.5),
        "b_scale": gain(ks[14], (N_EVEN, B_WIDTH)),
        "w_in_odd": nrm(ks[15], (N_ODD, D_MODEL, ODD_IN), D_MODEL ** -0.5),
        "w_out_odd": nrm(ks[16], (N_ODD, OUT_WIDTH, D_MODEL), 0.5 * OUT_WIDTH ** -0.5),
        "c_qnorm": gain(ks[17], (N_ODD, HEAD_DIM)),
        "c_knorm": gain(ks[18], (N_ODD, HEAD_DIM)),
        "c_sink": nrm(ks[19], (N_ODD, C_Q_HEADS), 1.0),
        "d_dw": nrm(ks[20], (N_ODD, CONV_WIDTH, D_WIDTH), CONV_WIDTH ** -0.5),
        "d_dw_b": nrm(ks[21], (N_ODD, D_WIDTH), 0.02),
        "d_ln_g": gain(ks[22], (N_ODD, D_WIDTH)),
        "d_ln_b": nrm(ks[23], (N_ODD, D_WIDTH), 0.02),
        "d_pw": nrm(ks[24], (N_ODD, D_WIDTH, D_WIDTH), D_WIDTH ** -0.5),
    }


def reference(x, mem, norm_g, mem_norm_g, w_mem_kv, m_qnorm, m_knorm,
              w_in_even, w_out_even, a_ln_g, a_ln_b, a_ws, a_bs, b_w, b_scale,
              w_in_odd, w_out_odd, c_qnorm, c_knorm, c_sink,
              d_dw, d_dw_b, d_ln_g, d_ln_b, d_pw):
    mem_n = rms_norm(mem, mem_norm_g)
    h = x
    for layer in range(DEPTH):
        i = layer // 2
        xn = rms_norm(h, norm_g[layer])
        if layer % 2 == 0:
            z = xn @ w_in_even[i]
            u, v, ga, xb, gb, qm, gm = _split(z, EVEN_SIZES)
            ya = chunked_sgu(u, v, a_ln_g[i], a_ln_b[i], a_ws[i], a_bs[i]) * jax.nn.silu(ga)
            yb = multiscale_pool(xb, b_w[i], b_scale[i]) * jax.nn.silu(gb)
            ym = memory_attention(qm, mem_n, w_mem_kv[layer], m_qnorm[layer], m_knorm[layer]) * jax.nn.silu(gm)
            y = jnp.concatenate([ya, yb, ym], axis=-1) @ w_out_even[i]
        else:
            z = xn @ w_in_odd[i]
            qc, kc, vc, gc, da, db, gd, qm, gm = _split(z, ODD_SIZES)
            yc = swa_sink_attention(qc, kc, vc, c_qnorm[i], c_knorm[i], c_sink[i]) * jax.nn.silu(gc)
            yd = conformer_conv(da, db, d_dw[i], d_dw_b[i], d_ln_g[i], d_ln_b[i], d_pw[i]) * jax.nn.silu(gd)
            ym = memory_attention(qm, mem_n, w_mem_kv[layer], m_qnorm[layer], m_knorm[layer]) * jax.nn.silu(gm)
            y = jnp.concatenate([yc, yd, ym], axis=-1) @ w_out_odd[i]
        h = h + y
    return h
```

```python
import functools

import jax
import jax.numpy as jnp
from jax import lax
from jax.experimental import pallas as pl
from jax.experimental.pallas import tpu as pltpu

D_MODEL = 1024
N_MEM = 256
HEAD_DIM = 64
EPS = 1e-6
A_WIDTH = 512
CHUNK = 128
POOL_WINDOWS = (2, 4, 8, 16)
B_WIDTH = 512
C_Q_WIDTH = 512
C_KV_WIDTH = 128
BLOCK = 128
D_WIDTH = 512
CONV_WIDTH = 31
M_HEADS = 4
M_WIDTH = 256
OUT_WIDTH = 1280
EVEN_IN = 3072
ODD_IN = 3328

LANES = 128
POOL_HALO = 16
CONV_HALO = 32
TILE = 256
VMEM_LIMIT = 56 * 1024 * 1024

F32 = jnp.float32
BF16 = jnp.bfloat16

E_U, E_V, E_GA, E_XB, E_GB, E_QM, E_GM = 0, 512, 1024, 1536, 2048, 2560, 2816
O_QC, O_KC, O_VC, O_GC, O_DA, O_DB, O_GD, O_QM, O_GM = 0, 512, 640, 768, 1280, 1792, 2304, 2816, 3072


def _dot(a, b):
    return jnp.dot(a, b, preferred_element_type=F32)


def _dot_nt(a, b):
    return lax.dot_general(a, b, (((1,), (1,)), ((), ())), preferred_element_type=F32)


def _silu(x):
    return x * jax.nn.sigmoid(x)


def _lo_lanes(shape):
    return lax.broadcasted_iota(jnp.int32, shape, len(shape) - 1) < HEAD_DIM


def _pair_rs(x):
    lo = _lo_lanes(x.shape)
    x2 = x * x
    s_lo = jnp.sum(jnp.where(lo, x2, 0.0), axis=-1, keepdims=True)
    s_hi = jnp.sum(jnp.where(lo, 0.0, x2), axis=-1, keepdims=True)
    return lax.rsqrt(jnp.where(lo, s_lo, s_hi) * (1.0 / HEAD_DIM) + EPS)


def _layer_norm(x, g, b):
    mu = jnp.mean(x, axis=-1, keepdims=True)
    xc = x - mu
    return xc * lax.rsqrt(jnp.mean(xc * xc, axis=-1, keepdims=True) + EPS) * g + b


def _project_in(h_ref, g_ref, win_ref, xn_s, z_s):
    h = h_ref[...]
    xn = h * lax.rsqrt(jnp.mean(h * h, axis=-1, keepdims=True) + EPS) * g_ref[...]
    xn_s[...] = xn.astype(BF16)
    z_s[...] = _dot(xn_s[...], win_ref[...])


def _memory_attention(z_s, q_off, g_off, kt_ref, vm_ref, y_s, y_off):
    for s in range(M_WIDTH // LANES):
        q = z_s[:, q_off + LANES * s:q_off + LANES * (s + 1)]
        qn = (q * _pair_rs(q)).astype(BF16)
        acc = None
        for pos in range(2):
            hd = 2 * s + pos
            sc = _dot(qn, kt_ref[hd])
            m = jnp.max(sc, axis=-1, keepdims=True)
            e = jnp.exp(sc - m)
            l = jnp.sum(e, axis=-1, keepdims=True)
            o = _dot(e.astype(BF16), vm_ref[hd]) * (1.0 / l)
            acc = o if acc is None else acc + o
        gate = _silu(z_s[:, g_off + LANES * s:g_off + LANES * (s + 1)])
        y_s[:, y_off + LANES * s:y_off + LANES * (s + 1)] = (acc * gate).astype(BF16)


def _even_kernel(h_ref, g_ref, win_ref, wout_ref, lng_ref, lnb_ref, w2_ref, abias_ref, bw_ref,
                 bscale_ref, kt_ref, vm_ref, o_ref, xn_s, z_s, y_s, sg_s, xb_s):
    t = pl.program_id(1)
    tile = h_ref.shape[0]
    nc = tile // CHUNK

    @pl.when(t == 0)
    def _():
        xb_s[0:POOL_HALO, :] = jnp.zeros((POOL_HALO, B_WIDTH), F32)

    _project_in(h_ref, g_ref, win_ref, xn_s, z_s)

    vn = _layer_norm(z_s[:, E_V:E_V + A_WIDTH], lng_ref[...], lnb_ref[...])
    lo = _lo_lanes((CHUNK, LANES))
    row = lax.broadcasted_iota(jnp.int32, (CHUNK, 2 * CHUNK), 0)
    col = lax.broadcasted_iota(jnp.int32, (CHUNK, 2 * CHUNK), 1)
    tril2 = (col & (CHUNK - 1)) <= row
    for s in range(A_WIDTH // LANES):
        tops, bots = [], []
        for c in range(nc):
            blk = vn[CHUNK * c:CHUNK * (c + 1), LANES * s:LANES * (s + 1)]
            tops.append(jnp.where(lo, blk, 0.0).astype(BF16))
            bots.append(jnp.where(lo, 0.0, blk).astype(BF16))
        rhs = jnp.concatenate([jnp.concatenate(tops, axis=1), jnp.concatenate(bots, axis=1)], axis=0)
        w2 = jnp.where(tril2, w2_ref[s], jnp.zeros((), BF16))
        sg = _dot(w2, rhs)
        for c in range(nc):
            sg_s[CHUNK * c:CHUNK * (c + 1), LANES * s:LANES * (s + 1)] = (
                sg[:, LANES * c:LANES * (c + 1)] + abias_ref[s])
    ya = z_s[:, E_U:E_U + A_WIDTH] * sg_s[...] * _silu(z_s[:, E_GA:E_GA + A_WIDTH])
    y_s[:, 0:A_WIDTH] = ya.astype(BF16)

    xb_s[POOL_HALO:POOL_HALO + tile, :] = z_s[:, E_XB:E_XB + B_WIDTH]
    t_abs = t * tile + lax.broadcasted_iota(jnp.int32, (tile, 1), 0)
    for g, w in enumerate(POOL_WINDOWS):
        win = xb_s[:, LANES * g:LANES * (g + 1)]
        acc = win
        k = 1
        while k < w:
            acc = acc + pltpu.roll(acc, k, axis=0)
            k *= 2
        cnt = jnp.minimum(t_abs + 1, w).astype(F32)
        pooled = acc[POOL_HALO:, :] * (1.0 / cnt)
        diff = pooled - win[POOL_HALO:, :]
        yb = _dot(diff.astype(BF16), bw_ref[g]) * bscale_ref[:, LANES * g:LANES * (g + 1)]
        yb = yb * _silu(z_s[:, E_GB + LANES * g:E_GB + LANES * (g + 1)])
        y_s[:, A_WIDTH + LANES * g:A_WIDTH + LANES * (g + 1)] = yb.astype(BF16)
    xb_s[0:POOL_HALO, :] = xb_s[tile:tile + POOL_HALO, :]

    _memory_attention(z_s, E_QM, E_GM, kt_ref, vm_ref, y_s, A_WIDTH + B_WIDTH)

    o_ref[...] = h_ref[...] + _dot(y_s[...], wout_ref[...])


def _odd_kernel(h_ref, g_ref, win_ref, wout_ref, qg_ref, kg_ref, sink_ref, dw_ref, dwb_ref, dlg_ref,
                dlb_ref, pw_ref, kt_ref, vm_ref, o_ref, xn_s, z_s, y_s, cv_s, hb_s, kb_s, vb_s):
    t = pl.program_id(1)
    tile = h_ref.shape[0]
    nb = tile // BLOCK

    @pl.when(t == 0)
    def _():
        hb_s[0:CONV_HALO, :] = jnp.zeros((CONV_HALO, D_WIDTH), F32)
        kb_s[:, 0:BLOCK, :] = jnp.zeros((4, BLOCK, LANES), BF16)
        vb_s[:, 0:BLOCK, :] = jnp.zeros((4, BLOCK, LANES), BF16)

    _project_in(h_ref, g_ref, win_ref, xn_s, z_s)

    lo = _lo_lanes((tile, LANES))
    k = z_s[:, O_KC:O_KC + C_KV_WIDTH]
    kn = k * _pair_rs(k) * (kg_ref[...] * qg_ref[...] * (HEAD_DIM ** -0.5))
    v = z_s[:, O_VC:O_VC + C_KV_WIDTH]
    for buf, x in ((kb_s, kn), (vb_s, v)):
        xsw = pltpu.roll(x, HEAD_DIM, axis=1)
        buf[0, BLOCK:BLOCK + tile, :] = jnp.where(lo, x, 0.0).astype(BF16)
        buf[1, BLOCK:BLOCK + tile, :] = jnp.where(lo, 0.0, x).astype(BF16)
        buf[2, BLOCK:BLOCK + tile, :] = jnp.where(lo, xsw, 0.0).astype(BF16)
        buf[3, BLOCK:BLOCK + tile, :] = jnp.where(lo, 0.0, xsw).astype(BF16)
    variant = ((0, 3), (2, 1))

    qi = lax.broadcasted_iota(jnp.int32, (BLOCK, 2 * BLOCK), 0)
    cj = lax.broadcasted_iota(jnp.int32, (BLOCK, 2 * BLOCK), 1)
    band = (cj - qi >= 1) & (cj - qi <= BLOCK)
    first_key = jnp.where(t == 0, BLOCK, 0)
    neg = jnp.finfo(F32).min
    for j in range(nb):
        valid = band & (cj >= first_key) if j == 0 else band
        for s in range(C_Q_WIDTH // LANES):
            q = z_s[BLOCK * j:BLOCK * (j + 1), O_QC + LANES * s:O_QC + LANES * (s + 1)]
            qn = (q * _pair_rs(q)).astype(BF16)
            acc = None
            for pos in range(2):
                hd = 2 * s + pos
                var = variant[hd // 4][pos]
                sc = _dot_nt(qn, kb_s[var, BLOCK * j:BLOCK * (j + 2), :])
                sc = jnp.where(valid, sc, neg)
                sink = sink_ref[hd]
                m = jnp.maximum(jnp.max(sc, axis=-1, keepdims=True), sink)
                e = jnp.exp(sc - m)
                l = jnp.sum(e, axis=-1, keepdims=True) + jnp.exp(sink - m)
                o = _dot(e.astype(BF16), vb_s[var, BLOCK * j:BLOCK * (j + 2), :]) * (1.0 / l)
                acc = o if acc is None else acc + o
            gate = _silu(z_s[BLOCK * j:BLOCK * (j + 1), O_GC + LANES * s:O_GC + LANES * (s + 1)])
            y_s[BLOCK * j:BLOCK * (j + 1), LANES * s:LANES * (s + 1)] = (acc * gate).astype(BF16)
    kb_s[:, 0:BLOCK, :] = kb_s[:, tile:tile + BLOCK, :]
    vb_s[:, 0:BLOCK, :] = vb_s[:, tile:tile + BLOCK, :]

    hb_s[CONV_HALO:CONV_HALO + tile, :] = (
        z_s[:, O_DA:O_DA + D_WIDTH] * jax.nn.sigmoid(z_s[:, O_DB:O_DB + D_WIDTH]))
    base = CONV_HALO - (CONV_WIDTH - 1)
    for s in range(D_WIDTH // LANES):
        for j in range(nb):
            acc = jnp.broadcast_to(dwb_ref[:, LANES * s:LANES * (s + 1)], (BLOCK, LANES))
            for tap in range(CONV_WIDTH):
                x = hb_s[BLOCK * j + base + tap:BLOCK * (j + 1) + base + tap, LANES * s:LANES * (s + 1)]
                acc = acc + x * dw_ref[tap:tap + 1, LANES * s:LANES * (s + 1)]
            cv_s[BLOCK * j:BLOCK * (j + 1), LANES * s:LANES * (s + 1)] = acc
    hb_s[0:CONV_HALO, :] = hb_s[tile:tile + CONV_HALO, :]
    hd_ = _silu(_layer_norm(cv_s[...], dlg_ref[...], dlb_ref[...]))
    yd = _dot(hd_.astype(BF16), pw_ref[...]) * _silu(z_s[:, O_GD:O_GD + D_WIDTH])
    y_s[:, C_Q_WIDTH:C_Q_WIDTH + D_WIDTH] = yd.astype(BF16)

    _memory_attention(z_s, O_QM, O_GM, kt_ref, vm_ref, y_s, C_Q_WIDTH + D_WIDTH)

    o_ref[...] = h_ref[...] + _dot(y_s[...], wout_ref[...])


def _mem_prep_kernel(mem_ref, g_ref, w_ref, qn_ref, kn_ref, kt_ref, vm_ref):
    m = mem_ref[...]
    mn = m * lax.rsqrt(jnp.mean(m * m, axis=-1, keepdims=True) + EPS) * g_ref[...]
    kv = _dot(mn.astype(BF16), w_ref[...])
    kt = kv[:, 0:M_WIDTH].T
    gain = kn_ref[...] * qn_ref[...] * (HEAD_DIM ** -0.5)
    zeros = jnp.zeros((HEAD_DIM, N_MEM), F32)
    lo = _lo_lanes((N_MEM, LANES))
    for hd in range(M_HEADS):
        kh = kt[HEAD_DIM * hd:HEAD_DIM * (hd + 1), :]
        kh = kh * lax.rsqrt(jnp.mean(kh * kh, axis=0, keepdims=True) + EPS) * gain
        parts = [kh, zeros] if hd % 2 == 0 else [zeros, kh]
        kt_ref[hd] = jnp.concatenate(parts, axis=0).astype(BF16)
        s = hd // 2
        v = kv[:, M_WIDTH + LANES * s:M_WIDTH + LANES * (s + 1)]
        vm_ref[hd] = (jnp.where(lo, v, 0.0) if hd % 2 == 0 else jnp.where(lo, 0.0, v)).astype(BF16)


def _full(shape):
    return pl.BlockSpec(shape, lambda b, t: (0,) * len(shape))


def _mem_prep(mem, mem_norm_g, w_mem_kv, m_qnorm, m_knorm):
    depth, batch = w_mem_kv.shape[0], mem.shape[0]
    return pl.pallas_call(
        _mem_prep_kernel,
        grid=(depth, batch),
        in_specs=[
            pl.BlockSpec((None, N_MEM, D_MODEL), lambda l, b: (b, 0, 0)),
            pl.BlockSpec((1, D_MODEL), lambda l, b: (0, 0)),
            pl.BlockSpec((None, D_MODEL, 2 * M_WIDTH), lambda l, b: (l, 0, 0)),
            pl.BlockSpec((None, HEAD_DIM, 1), lambda l, b: (l, 0, 0)),
            pl.BlockSpec((None, HEAD_DIM, 1), lambda l, b: (l, 0, 0)),
        ],
        out_specs=[
            pl.BlockSpec((None, None, M_HEADS, LANES, N_MEM), lambda l, b: (l, b, 0, 0, 0)),
            pl.BlockSpec((None, None, M_HEADS, N_MEM, LANES), lambda l, b: (l, b, 0, 0, 0)),
        ],
        out_shape=[
            jax.ShapeDtypeStruct((depth, batch, M_HEADS, LANES, N_MEM), BF16),
            jax.ShapeDtypeStruct((depth, batch, M_HEADS, N_MEM, LANES), BF16),
        ],
        compiler_params=pltpu.CompilerParams(dimension_semantics=("arbitrary", "arbitrary")),
        name="mem_prep",
    )(mem, mem_norm_g.reshape(1, D_MODEL), w_mem_kv.astype(BF16),
      m_qnorm.reshape(depth, HEAD_DIM, 1), m_knorm.reshape(depth, HEAD_DIM, 1))


def _mem_specs(layer):
    return [
        pl.BlockSpec((None, None, M_HEADS, LANES, N_MEM), lambda b, t: (layer, b, 0, 0, 0)),
        pl.BlockSpec((None, None, M_HEADS, N_MEM, LANES), lambda b, t: (layer, b, 0, 0, 0)),
    ]


def _layer_call(body, name, h, in_width, operands, specs, scratch, layer, kt, vm):
    batch, seq, _ = h.shape
    h_spec = pl.BlockSpec((None, TILE, D_MODEL), lambda b, t: (b, t, 0))
    return pl.pallas_call(
        body,
        grid=(batch, seq // TILE),
        in_specs=[h_spec] + specs + _mem_specs(layer),
        out_specs=h_spec,
        out_shape=jax.ShapeDtypeStruct(h.shape, h.dtype),
        scratch_shapes=[
            pltpu.VMEM((TILE, D_MODEL), BF16),
            pltpu.VMEM((TILE, in_width), F32),
            pltpu.VMEM((TILE, OUT_WIDTH), BF16),
        ] + scratch,
        compiler_params=pltpu.CompilerParams(
            dimension_semantics=("arbitrary", "arbitrary"), vmem_limit_bytes=VMEM_LIMIT),
        name=name,
    )(h, *operands, kt, vm)


def _even_layer(h, layer, kt, vm, norm_g, w_in, w_out, ln_g, ln_b, ws, bs, bw, bscale):
    heads = ws.shape[0]
    w2 = ws.reshape(heads // 2, 2, CHUNK, CHUNK).transpose(0, 2, 1, 3).reshape(heads // 2, CHUNK, 2 * CHUNK)
    abias = jnp.repeat(bs.reshape(heads // 2, 2, CHUNK).transpose(0, 2, 1), HEAD_DIM, axis=2)
    operands = [norm_g.reshape(1, D_MODEL), w_in.astype(BF16), w_out.astype(BF16),
                ln_g.reshape(1, A_WIDTH), ln_b.reshape(1, A_WIDTH), w2.astype(BF16), abias,
                bw.astype(BF16), bscale.reshape(1, B_WIDTH)]
    specs = [_full((1, D_MODEL)), _full((D_MODEL, EVEN_IN)), _full((OUT_WIDTH, D_MODEL)),
             _full((1, A_WIDTH)), _full((1, A_WIDTH)), _full((heads // 2, CHUNK, 2 * CHUNK)),
             _full((heads // 2, CHUNK, LANES)), _full((len(POOL_WINDOWS), LANES, LANES)),
             _full((1, B_WIDTH))]
    scratch = [pltpu.VMEM((TILE, A_WIDTH), F32),
               pltpu.VMEM((POOL_HALO + TILE, B_WIDTH), F32)]
    return _layer_call(_even_kernel, "even_layer", h, EVEN_IN, operands, specs, scratch, layer, kt, vm)


def _odd_layer(h, layer, kt, vm, norm_g, w_in, w_out, qn, kn, sink, dw, dw_b, ln_g, ln_b, pw):
    operands = [norm_g.reshape(1, D_MODEL), w_in.astype(BF16), w_out.astype(BF16),
                jnp.tile(qn, 2).reshape(1, LANES), jnp.tile(kn, 2).reshape(1, LANES), sink,
                dw, dw_b.reshape(1, D_WIDTH), ln_g.reshape(1, D_WIDTH), ln_b.reshape(1, D_WIDTH),
                pw.astype(BF16)]
    specs = [_full((1, D_MODEL)), _full((D_MODEL, ODD_IN)), _full((OUT_WIDTH, D_MODEL)),
             _full((1, LANES)), _full((1, LANES)), pl.BlockSpec(memory_space=pltpu.SMEM),
             _full((CONV_WIDTH, D_WIDTH)), _full((1, D_WIDTH)), _full((1, D_WIDTH)), _full((1, D_WIDTH)),
             _full((D_WIDTH, D_WIDTH))]
    scratch = [pltpu.VMEM((TILE, D_WIDTH), F32),
               pltpu.VMEM((CONV_HALO + TILE, D_WIDTH), F32),
               pltpu.VMEM((4, BLOCK + TILE, LANES), BF16),
               pltpu.VMEM((4, BLOCK + TILE, LANES), BF16)]
    return _layer_call(_odd_kernel, "odd_layer", h, ODD_IN, operands, specs, scratch, layer, kt, vm)


def kernel(x, mem, norm_g, mem_norm_g, w_mem_kv, m_qnorm, m_knorm, w_in_even, w_out_even, a_ln_g, a_ln_b, a_ws, a_bs, b_w, b_scale, w_in_odd, w_out_odd, c_qnorm, c_knorm, c_sink, d_dw, d_dw_b, d_ln_g, d_ln_b, d_pw):
    kt, vm = _mem_prep(mem, mem_norm_g, w_mem_kv, m_qnorm, m_knorm)
    h = x
    for layer in range(w_mem_kv.shape[0]):
        i = layer // 2
        if layer % 2 == 0:
            h = _even_layer(h, layer, kt, vm, norm_g[layer], w_in_even[i], w_out_even[i], a_ln_g[i],
                            a_ln_b[i], a_ws[i], a_bs[i], b_w[i], b_scale[i])
        else:
            h = _odd_layer(h, layer, kt, vm, norm_g[layer], w_in_odd[i], w_out_odd[i], c_qnorm[i],
                           c_knorm[i], c_sink[i], d_dw[i], d_dw_b[i], d_ln_g[i], d_ln_b[i], d_pw[i])
    return h
```

```python
import functools

import jax
import jax.numpy as jnp
from jax import lax
from jax.experimental import pallas as pl
from jax.experimental.pallas import tpu as pltpu

D_MODEL = 1024
N_MEM = 256
HEAD_DIM = 64
EPS = 1e-6
A_WIDTH = 512
CHUNK = 128
POOL_WINDOWS = (2, 4, 8, 16)
B_WIDTH = 512
C_Q_WIDTH = 512
C_KV_WIDTH = 128
BLOCK = 128
D_WIDTH = 512
CONV_WIDTH = 31
M_HEADS = 4
M_WIDTH = 256
OUT_WIDTH = 1280
EVEN_IN = 3072
ODD_IN = 3328

LANES = 128
SUBLANES = 8
POOL_HALO = 16
CONV_HALO = 32
TILE = 512
VMEM_LIMIT = 56 * 1024 * 1024

F32 = jnp.float32
BF16 = jnp.bfloat16

E_U, E_V, E_GA, E_XB, E_GB, E_QM, E_GM = 0, 512, 1024, 1536, 2048, 2560, 2816
O_QC, O_KC, O_VC, O_GC, O_DA, O_DB, O_GD, O_QM, O_GM = 0, 512, 640, 768, 1280, 1792, 2304, 2816, 3072


def _dot(a, b):
    return jnp.dot(a, b, preferred_element_type=F32)


def _dot_nt(a, b):
    return lax.dot_general(a, b, (((1,), (1,)), ((), ())), preferred_element_type=F32)


def _silu(x):
    return x * jax.nn.sigmoid(x)


def _lo_lanes(shape):
    return lax.broadcasted_iota(jnp.int32, shape, len(shape) - 1) < HEAD_DIM


def _pair_rs(x):
    lo = _lo_lanes(x.shape)
    x2 = x * x
    s_lo = jnp.sum(jnp.where(lo, x2, 0.0), axis=-1, keepdims=True)
    s_hi = jnp.sum(jnp.where(lo, 0.0, x2), axis=-1, keepdims=True)
    return lax.rsqrt(jnp.where(lo, s_lo, s_hi) * (1.0 / HEAD_DIM) + EPS)


def _layer_norm(x, g, b):
    mu = jnp.mean(x, axis=-1, keepdims=True)
    xc = x - mu
    return xc * lax.rsqrt(jnp.mean(xc * xc, axis=-1, keepdims=True) + EPS) * g + b


def _project_in(h_ref, g_ref, win_ref, xn_s, z_s, groups):
    h = h_ref[...]
    xn = h * lax.rsqrt(jnp.mean(h * h, axis=-1, keepdims=True) + EPS) * g_ref[...]
    xn_s[...] = xn.astype(BF16)
    for lo, hi in groups:
        z_s[:, lo:hi] = _dot(xn_s[...], win_ref[:, lo:hi])


def _memory_attention(z_s, q_off, g_off, kt_ref, vm_ref, y_s, y_off):
    for s in range(M_WIDTH // LANES):
        q = z_s[:, q_off + LANES * s:q_off + LANES * (s + 1)]
        qn = (q * _pair_rs(q)).astype(BF16)
        acc = None
        for pos in range(2):
            hd = 2 * s + pos
            sc = _dot(qn, kt_ref[hd])
            m = jnp.max(sc, axis=-1, keepdims=True)
            e = jnp.exp(sc - m)
            l = jnp.sum(e, axis=-1, keepdims=True)
            o = _dot(e.astype(BF16), vm_ref[hd]) * (1.0 / l)
            acc = o if acc is None else acc + o
        gate = _silu(z_s[:, g_off + LANES * s:g_off + LANES * (s + 1)])
        y_s[:, y_off + LANES * s:y_off + LANES * (s + 1)] = (acc * gate).astype(BF16)


def _even_kernel(h_ref, g_ref, win_ref, wout_ref, lng_ref, lnb_ref, w2_ref, abias_ref, bw_ref,
                 bscale_ref, kt_ref, vm_ref, o_ref, xn_s, z_s, y_s, sg_s, xb_s):
    t = pl.program_id(1)
    tile = h_ref.shape[0]
    nc = tile // CHUNK

    @pl.when(t == 0)
    def _():
        xb_s[0:POOL_HALO, :] = jnp.zeros((POOL_HALO, B_WIDTH), F32)

    _project_in(h_ref, g_ref, win_ref, xn_s, z_s, ((E_U, E_XB), (E_XB, E_QM), (E_QM, EVEN_IN)))

    vn = _layer_norm(z_s[:, E_V:E_V + A_WIDTH], lng_ref[...], lnb_ref[...])
    lo = _lo_lanes((CHUNK, LANES))
    row = lax.broadcasted_iota(jnp.int32, (CHUNK, 2 * CHUNK), 0)
    col = lax.broadcasted_iota(jnp.int32, (CHUNK, 2 * CHUNK), 1)
    tril2 = (col & (CHUNK - 1)) <= row
    for s in range(A_WIDTH // LANES):
        tops, bots = [], []
        for c in range(nc):
            blk = vn[CHUNK * c:CHUNK * (c + 1), LANES * s:LANES * (s + 1)]
            tops.append(jnp.where(lo, blk, 0.0).astype(BF16))
            bots.append(jnp.where(lo, 0.0, blk).astype(BF16))
        rhs = jnp.concatenate([jnp.concatenate(tops, axis=1), jnp.concatenate(bots, axis=1)], axis=0)
        w2 = jnp.where(tril2, w2_ref[s], jnp.zeros((), BF16))
        sg = _dot(w2, rhs)
        for c in range(nc):
            sg_s[CHUNK * c:CHUNK * (c + 1), LANES * s:LANES * (s + 1)] = (
                sg[:, LANES * c:LANES * (c + 1)] + abias_ref[s])
    ya = z_s[:, E_U:E_U + A_WIDTH] * sg_s[...] * _silu(z_s[:, E_GA:E_GA + A_WIDTH])
    y_s[:, 0:A_WIDTH] = ya.astype(BF16)

    xb_s[POOL_HALO:POOL_HALO + tile, :] = z_s[:, E_XB:E_XB + B_WIDTH]
    t_abs = t * tile + lax.broadcasted_iota(jnp.int32, (tile, 1), 0)
    for g, w in enumerate(POOL_WINDOWS):
        win = xb_s[:, LANES * g:LANES * (g + 1)]
        acc = win
        k = 1
        while k < w:
            acc = acc + pltpu.roll(acc, k, axis=0)
            k *= 2
        cnt = jnp.minimum(t_abs + 1, w).astype(F32)
        pooled = acc[POOL_HALO:, :] * (1.0 / cnt)
        diff = pooled - win[POOL_HALO:, :]
        yb = _dot(diff.astype(BF16), bw_ref[g]) * bscale_ref[:, LANES * g:LANES * (g + 1)]
        yb = yb * _silu(z_s[:, E_GB + LANES * g:E_GB + LANES * (g + 1)])
        y_s[:, A_WIDTH + LANES * g:A_WIDTH + LANES * (g + 1)] = yb.astype(BF16)
    xb_s[0:POOL_HALO, :] = xb_s[tile:tile + POOL_HALO, :]

    _memory_attention(z_s, E_QM, E_GM, kt_ref, vm_ref, y_s, A_WIDTH + B_WIDTH)

    o_ref[...] = h_ref[...] + _dot(y_s[...], wout_ref[...])


def _odd_kernel(h_ref, g_ref, win_ref, wout_ref, qg_ref, kg_ref, sink_ref, dw_ref, dwb_ref, dlg_ref,
                dlb_ref, pw_ref, kt_ref, vm_ref, o_ref, xn_s, z_s, y_s, cv_s, hb_s, kb_s, vb_s):
    t = pl.program_id(1)
    tile = h_ref.shape[0]
    nb = tile // BLOCK

    @pl.when(t == 0)
    def _():
        hb_s[0, 0:CONV_HALO, :] = jnp.zeros((CONV_HALO, D_WIDTH), F32)
        kb_s[:, 0:BLOCK, :] = jnp.zeros((4, BLOCK, LANES), BF16)
        vb_s[:, 0:BLOCK, :] = jnp.zeros((4, BLOCK, LANES), BF16)

    _project_in(h_ref, g_ref, win_ref, xn_s, z_s, ((O_DA, O_QM), (O_QC, O_DA), (O_QM, ODD_IN)))

    lo = _lo_lanes((tile, LANES))
    k = z_s[:, O_KC:O_KC + C_KV_WIDTH]
    kn = k * _pair_rs(k) * (kg_ref[...] * qg_ref[...] * (HEAD_DIM ** -0.5))
    v = z_s[:, O_VC:O_VC + C_KV_WIDTH]
    for buf, x in ((kb_s, kn), (vb_s, v)):
        xsw = pltpu.roll(x, HEAD_DIM, axis=1)
        buf[0, BLOCK:BLOCK + tile, :] = jnp.where(lo, x, 0.0).astype(BF16)
        buf[1, BLOCK:BLOCK + tile, :] = jnp.where(lo, 0.0, x).astype(BF16)
        buf[2, BLOCK:BLOCK + tile, :] = jnp.where(lo, xsw, 0.0).astype(BF16)
        buf[3, BLOCK:BLOCK + tile, :] = jnp.where(lo, 0.0, xsw).astype(BF16)
    variant = ((0, 3), (2, 1))

    qi = lax.broadcasted_iota(jnp.int32, (BLOCK, 2 * BLOCK), 0)
    cj = lax.broadcasted_iota(jnp.int32, (BLOCK, 2 * BLOCK), 1)
    band = (cj - qi >= 1) & (cj - qi <= BLOCK)
    first_key = jnp.where(t == 0, BLOCK, 0)
    neg = jnp.finfo(F32).min
    for j in range(nb):
        valid = band & (cj >= first_key) if j == 0 else band
        for s in range(C_Q_WIDTH // LANES):
            q = z_s[BLOCK * j:BLOCK * (j + 1), O_QC + LANES * s:O_QC + LANES * (s + 1)]
            qn = (q * _pair_rs(q)).astype(BF16)
            acc = None
            for pos in range(2):
                hd = 2 * s + pos
                var = variant[hd // 4][pos]
                sc = _dot_nt(qn, kb_s[var, BLOCK * j:BLOCK * (j + 2), :])
                sc = jnp.where(valid, sc, neg)
                sink = sink_ref[hd]
                m = jnp.maximum(jnp.max(sc, axis=-1, keepdims=True), sink)
                e = jnp.exp(sc - m)
                l = jnp.sum(e, axis=-1, keepdims=True) + jnp.exp(sink - m)
                o = _dot(e.astype(BF16), vb_s[var, BLOCK * j:BLOCK * (j + 2), :]) * (1.0 / l)
                acc = o if acc is None else acc + o
            gate = _silu(z_s[BLOCK * j:BLOCK * (j + 1), O_GC + LANES * s:O_GC + LANES * (s + 1)])
            y_s[BLOCK * j:BLOCK * (j + 1), LANES * s:LANES * (s + 1)] = (acc * gate).astype(BF16)
    kb_s[:, 0:BLOCK, :] = kb_s[:, tile:tile + BLOCK, :]
    vb_s[:, 0:BLOCK, :] = vb_s[:, tile:tile + BLOCK, :]

    hb_s[0, CONV_HALO:CONV_HALO + tile, :] = (
        z_s[:, O_DA:O_DA + D_WIDTH] * jax.nn.sigmoid(z_s[:, O_DB:O_DB + D_WIDTH]))
    shifted_rows = CONV_HALO + tile - SUBLANES
    for p in range(1, SUBLANES):
        hb_s[p, 0:shifted_rows, :] = hb_s[0, p:p + shifted_rows, :]
    base = CONV_HALO - (CONV_WIDTH - 1)
    for s in range(D_WIDTH // LANES):
        for j in range(nb):
            acc = jnp.broadcast_to(dwb_ref[:, LANES * s:LANES * (s + 1)], (BLOCK, LANES))
            for tap in range(CONV_WIDTH):
                p, r = (base + tap) % SUBLANES, (base + tap) // SUBLANES * SUBLANES
                x = hb_s[p, BLOCK * j + r:BLOCK * (j + 1) + r, LANES * s:LANES * (s + 1)]
                acc = acc + x * dw_ref[tap:tap + 1, LANES * s:LANES * (s + 1)]
            cv_s[BLOCK * j:BLOCK * (j + 1), LANES * s:LANES * (s + 1)] = acc
    hb_s[0, 0:CONV_HALO, :] = hb_s[0, tile:tile + CONV_HALO, :]
    hd_ = _silu(_layer_norm(cv_s[...], dlg_ref[...], dlb_ref[...]))
    yd = _dot(hd_.astype(BF16), pw_ref[...]) * _silu(z_s[:, O_GD:O_GD + D_WIDTH])
    y_s[:, C_Q_WIDTH:C_Q_WIDTH + D_WIDTH] = yd.astype(BF16)

    _memory_attention(z_s, O_QM, O_GM, kt_ref, vm_ref, y_s, C_Q_WIDTH + D_WIDTH)

    o_ref[...] = h_ref[...] + _dot(y_s[...], wout_ref[...])


def _mem_prep_kernel(mem_ref, g_ref, w_ref, qn_ref, kn_ref, kt_ref, vm_ref):
    m = mem_ref[...]
    mn = m * lax.rsqrt(jnp.mean(m * m, axis=-1, keepdims=True) + EPS) * g_ref[...]
    kv = _dot(mn.astype(BF16), w_ref[...])
    kt = kv[:, 0:M_WIDTH].T
    gain = kn_ref[...] * qn_ref[...] * (HEAD_DIM ** -0.5)
    zeros = jnp.zeros((HEAD_DIM, N_MEM), F32)
    lo = _lo_lanes((N_MEM, LANES))
    for hd in range(M_HEADS):
        kh = kt[HEAD_DIM * hd:HEAD_DIM * (hd + 1), :]
        kh = kh * lax.rsqrt(jnp.mean(kh * kh, axis=0, keepdims=True) + EPS) * gain
        parts = [kh, zeros] if hd % 2 == 0 else [zeros, kh]
        kt_ref[hd] = jnp.concatenate(parts, axis=0).astype(BF16)
        s = hd // 2
        v = kv[:, M_WIDTH + LANES * s:M_WIDTH + LANES * (s + 1)]
        vm_ref[hd] = (jnp.where(lo, v, 0.0) if hd % 2 == 0 else jnp.where(lo, 0.0, v)).astype(BF16)


def _full(shape):
    return pl.BlockSpec(shape, lambda b, t: (0,) * len(shape))


def _mem_prep(mem, mem_norm_g, w_mem_kv, m_qnorm, m_knorm):
    depth, batch = w_mem_kv.shape[0], mem.shape[0]
    return pl.pallas_call(
        _mem_prep_kernel,
        grid=(depth, batch),
        in_specs=[
            pl.BlockSpec((None, N_MEM, D_MODEL), lambda l, b: (b, 0, 0)),
            pl.BlockSpec((1, D_MODEL), lambda l, b: (0, 0)),
            pl.BlockSpec((None, D_MODEL, 2 * M_WIDTH), lambda l, b: (l, 0, 0)),
            pl.BlockSpec((None, HEAD_DIM, 1), lambda l, b: (l, 0, 0)),
            pl.BlockSpec((None, HEAD_DIM, 1), lambda l, b: (l, 0, 0)),
        ],
        out_specs=[
            pl.BlockSpec((None, None, M_HEADS, LANES, N_MEM), lambda l, b: (l, b, 0, 0, 0)),
            pl.BlockSpec((None, None, M_HEADS, N_MEM, LANES), lambda l, b: (l, b, 0, 0, 0)),
        ],
        out_shape=[
            jax.ShapeDtypeStruct((depth, batch, M_HEADS, LANES, N_MEM), BF16),
            jax.ShapeDtypeStruct((depth, batch, M_HEADS, N_MEM, LANES), BF16),
        ],
        compiler_params=pltpu.CompilerParams(dimension_semantics=("arbitrary", "arbitrary")),
        name="mem_prep",
    )(mem, mem_norm_g.reshape(1, D_MODEL), w_mem_kv.astype(BF16),
      m_qnorm.reshape(depth, HEAD_DIM, 1), m_knorm.reshape(depth, HEAD_DIM, 1))


def _mem_specs(layer):
    return [
        pl.BlockSpec((None, None, M_HEADS, LANES, N_MEM), lambda b, t: (layer, b, 0, 0, 0)),
        pl.BlockSpec((None, None, M_HEADS, N_MEM, LANES), lambda b, t: (layer, b, 0, 0, 0)),
    ]


def _layer_call(body, name, h, in_width, operands, specs, scratch, layer, kt, vm):
    batch, seq, _ = h.shape
    h_spec = pl.BlockSpec((None, TILE, D_MODEL), lambda b, t: (b, t, 0))
    return pl.pallas_call(
        body,
        grid=(batch, seq // TILE),
        in_specs=[h_spec] + specs + _mem_specs(layer),
        out_specs=h_spec,
        out_shape=jax.ShapeDtypeStruct(h.shape, h.dtype),
        scratch_shapes=[
            pltpu.VMEM((TILE, D_MODEL), BF16),
            pltpu.VMEM((TILE, in_width), F32),
            pltpu.VMEM((TILE, OUT_WIDTH), BF16),
        ] + scratch,
        compiler_params=pltpu.CompilerParams(
            dimension_semantics=("arbitrary", "arbitrary"), vmem_limit_bytes=VMEM_LIMIT),
        name=name,
    )(h, *operands, kt, vm)


def _even_layer(h, layer, kt, vm, norm_g, w_in, w_out, ln_g, ln_b, ws, bs, bw, bscale):
    heads = ws.shape[0]
    w2 = ws.reshape(heads // 2, 2, CHUNK, CHUNK).transpose(0, 2, 1, 3).reshape(heads // 2, CHUNK, 2 * CHUNK)
    abias = jnp.repeat(bs.reshape(heads // 2, 2, CHUNK).transpose(0, 2, 1), HEAD_DIM, axis=2)
    operands = [norm_g.reshape(1, D_MODEL), w_in.astype(BF16), w_out.astype(BF16),
                ln_g.reshape(1, A_WIDTH), ln_b.reshape(1, A_WIDTH), w2.astype(BF16), abias,
                bw.astype(BF16), bscale.reshape(1, B_WIDTH)]
    specs = [_full((1, D_MODEL)), _full((D_MODEL, EVEN_IN)), _full((OUT_WIDTH, D_MODEL)),
             _full((1, A_WIDTH)), _full((1, A_WIDTH)), _full((heads // 2, CHUNK, 2 * CHUNK)),
             _full((heads // 2, CHUNK, LANES)), _full((len(POOL_WINDOWS), LANES, LANES)),
             _full((1, B_WIDTH))]
    scratch = [pltpu.VMEM((TILE, A_WIDTH), F32),
               pltpu.VMEM((POOL_HALO + TILE, B_WIDTH), F32)]
    return _layer_call(_even_kernel, "even_layer", h, EVEN_IN, operands, specs, scratch, layer, kt, vm)


def _odd_layer(h, layer, kt, vm, norm_g, w_in, w_out, qn, kn, sink, dw, dw_b, ln_g, ln_b, pw):
    operands = [norm_g.reshape(1, D_MODEL), w_in.astype(BF16), w_out.astype(BF16),
                jnp.tile(qn, 2).reshape(1, LANES), jnp.tile(kn, 2).reshape(1, LANES), sink,
                dw, dw_b.reshape(1, D_WIDTH), ln_g.reshape(1, D_WIDTH), ln_b.reshape(1, D_WIDTH),
                pw.astype(BF16)]
    specs = [_full((1, D_MODEL)), _full((D_MODEL, ODD_IN)), _full((OUT_WIDTH, D_MODEL)),
             _full((1, LANES)), _full((1, LANES)), pl.BlockSpec(memory_space=pltpu.SMEM),
             _full((CONV_WIDTH, D_WIDTH)), _full((1, D_WIDTH)), _full((1, D_WIDTH)), _full((1, D_WIDTH)),
             _full((D_WIDTH, D_WIDTH))]
    scratch = [pltpu.VMEM((TILE, D_WIDTH), F32),
               pltpu.VMEM((SUBLANES, CONV_HALO + TILE, D_WIDTH), F32),
               pltpu.VMEM((4, BLOCK + TILE, LANES), BF16),
               pltpu.VMEM((4, BLOCK + TILE, LANES), BF16)]
    return _layer_call(_odd_kernel, "odd_layer", h, ODD_IN, operands, specs, scratch, layer, kt, vm)


def kernel(x, mem, norm_g, mem_norm_g, w_mem_kv, m_qnorm, m_knorm, w_in_even, w_out_even, a_ln_g, a_ln_b, a_ws, a_bs, b_w, b_scale, w_in_odd, w_out_odd, c_qnorm, c_knorm, c_sink, d_dw, d_dw_b, d_ln_g, d_ln_b, d_pw):
    kt, vm = _mem_prep(mem, mem_norm_g, w_mem_kv, m_qnorm, m_knorm)
    h = x
    for layer in range(w_mem_kv.shape[0]):
        i = layer // 2
        if layer % 2 == 0:
            h = _even_layer(h, layer, kt, vm, norm_g[layer], w_in_even[i], w_out_even[i], a_ln_g[i],
                            a_ln_b[i], a_ws[i], a_bs[i], b_w[i], b_scale[i])
        else:
            h = _odd_layer(h, layer, kt, vm, norm_g[layer], w_in_odd[i], w_out_odd[i], c_qnorm[i],
                           c_knorm[i], c_sink[i], d_dw[i], d_dw_b[i], d_ln_g[i], d_ln_b[i], d_pw[i])
    return h
```

```python
import functools

import jax
import jax.numpy as jnp
from jax import lax
from jax.experimental import pallas as pl
from jax.experimental.pallas import tpu as pltpu

D_MODEL = 1024
N_MEM = 256
HEAD_DIM = 64
EPS = 1e-6
LOG2E = 1.4426950408889634
A_WIDTH = 512
CHUNK = 128
POOL_WINDOWS = (2, 4, 8, 16)
B_WIDTH = 512
C_Q_WIDTH = 512
C_KV_WIDTH = 128
BLOCK = 128
D_WIDTH = 512
CONV_WIDTH = 31
M_HEADS = 4
M_WIDTH = 256
OUT_WIDTH = 1280
EVEN_IN = 3072
ODD_IN = 3328

LANES = 128
SUBLANES = 8
POOL_HALO = 16
CONV_HALO = 32
TILE = 512
EVEN_SUB_TILES = 2
ODD_SUB_TILES = 1
VMEM_LIMIT = 56 * 1024 * 1024

F32 = jnp.float32
BF16 = jnp.bfloat16

E_U, E_V, E_GA, E_XB, E_GB, E_QM, E_GM = 0, 512, 1024, 1536, 2048, 2560, 2816
O_QC, O_KC, O_VC, O_GC, O_DA, O_DB, O_GD, O_QM, O_GM = 0, 512, 640, 768, 1280, 1792, 2304, 2816, 3072


def _dot(a, b):
    return jnp.dot(a, b, preferred_element_type=F32)


def _dot_nt(a, b):
    return lax.dot_general(a, b, (((1,), (1,)), ((), ())), preferred_element_type=F32)


def _sigmoid(x):
    return 0.5 * jnp.tanh(0.5 * x) + 0.5


def _silu(x):
    hx = 0.5 * x
    return hx * jnp.tanh(hx) + hx


def _lo_lanes(shape):
    return lax.broadcasted_iota(jnp.int32, shape, len(shape) - 1) < HEAD_DIM


def _pair_rs(x):
    lo = _lo_lanes(x.shape)
    x2 = x * x
    s_lo = jnp.sum(jnp.where(lo, x2, 0.0), axis=-1, keepdims=True)
    s_hi = jnp.sum(jnp.where(lo, 0.0, x2), axis=-1, keepdims=True)
    return lax.rsqrt(jnp.where(lo, s_lo, s_hi) * (1.0 / HEAD_DIM) + EPS)


def _layer_norm(x, g, b):
    mu = jnp.mean(x, axis=-1, keepdims=True)
    xc = x - mu
    return xc * lax.rsqrt(jnp.mean(xc * xc, axis=-1, keepdims=True) + EPS) * g + b


def _project_in(h_ref, g_ref, win_ref, xn_s, z_s, groups):
    h = h_ref[...]
    xn = h * lax.rsqrt(jnp.mean(h * h, axis=-1, keepdims=True) + EPS) * g_ref[...]
    xn_s[...] = xn.astype(BF16)
    for lo, hi in groups:
        z_s[:, lo:hi] = _dot(xn_s[...], win_ref[:, lo:hi])


def _memory_attention(z_s, q_off, g_off, kt_ref, vm_ref, y_s, y_off):
    for s in range(M_WIDTH // LANES):
        q = z_s[:, q_off + LANES * s:q_off + LANES * (s + 1)]
        qn = (q * _pair_rs(q)).astype(BF16)
        acc = None
        for pos in range(2):
            hd = 2 * s + pos
            sc = _dot(qn, kt_ref[hd])
            m = jnp.max(sc, axis=-1, keepdims=True)
            e = jnp.exp2(sc - m)
            l = jnp.sum(e, axis=-1, keepdims=True)
            o = _dot(e.astype(BF16), vm_ref[hd]) * (1.0 / l)
            acc = o if acc is None else acc + o
        gate = _silu(z_s[:, g_off + LANES * s:g_off + LANES * (s + 1)])
        y_s[:, y_off + LANES * s:y_off + LANES * (s + 1)] = (acc * gate).astype(BF16)


def _even_tile(t_abs0, h_ref, g_ref, win_ref, wout_ref, lng_ref, lnb_ref, w2_ref, abias_ref, bw_ref,
               bscale_ref, kt_ref, vm_ref, o_ref, xn_s, z_s, y_s, sg_s, xb_s, xb_prev):
    tile = h_ref.shape[0]
    nc = tile // CHUNK
    xb_s[0:POOL_HALO, :] = xb_prev[tile:tile + POOL_HALO, :]

    _project_in(h_ref, g_ref, win_ref, xn_s, z_s, ((E_U, E_XB), (E_XB, E_QM), (E_QM, EVEN_IN)))

    vn = _layer_norm(z_s[:, E_V:E_V + A_WIDTH], lng_ref[...], lnb_ref[...])
    lo = _lo_lanes((CHUNK, LANES))
    row = lax.broadcasted_iota(jnp.int32, (CHUNK, 2 * CHUNK), 0)
    col = lax.broadcasted_iota(jnp.int32, (CHUNK, 2 * CHUNK), 1)
    tril2 = (col & (CHUNK - 1)) <= row
    for s in range(A_WIDTH // LANES):
        tops, bots = [], []
        for c in range(nc):
            blk = vn[CHUNK * c:CHUNK * (c + 1), LANES * s:LANES * (s + 1)]
            tops.append(jnp.where(lo, blk, 0.0).astype(BF16))
            bots.append(jnp.where(lo, 0.0, blk).astype(BF16))
        rhs = jnp.concatenate([jnp.concatenate(tops, axis=1), jnp.concatenate(bots, axis=1)], axis=0)
        w2 = jnp.where(tril2, w2_ref[s], jnp.zeros((), BF16))
        sg = _dot(w2, rhs)
        for c in range(nc):
            sg_s[CHUNK * c:CHUNK * (c + 1), LANES * s:LANES * (s + 1)] = (
                sg[:, LANES * c:LANES * (c + 1)] + abias_ref[s])
    ya = z_s[:, E_U:E_U + A_WIDTH] * sg_s[...] * _silu(z_s[:, E_GA:E_GA + A_WIDTH])
    y_s[:, 0:A_WIDTH] = ya.astype(BF16)

    xb_s[POOL_HALO:POOL_HALO + tile, :] = z_s[:, E_XB:E_XB + B_WIDTH]
    t_abs = t_abs0 + lax.broadcasted_iota(jnp.int32, (tile, 1), 0)
    for g, w in enumerate(POOL_WINDOWS):
        win = xb_s[:, LANES * g:LANES * (g + 1)]
        acc = win
        k = 1
        while k < w:
            acc = acc + pltpu.roll(acc, k, axis=0)
            k *= 2
        cnt = jnp.minimum(t_abs + 1, w).astype(F32)
        pooled = acc[POOL_HALO:, :] * (1.0 / cnt)
        diff = pooled - win[POOL_HALO:, :]
        yb = _dot(diff.astype(BF16), bw_ref[g]) * bscale_ref[:, LANES * g:LANES * (g + 1)]
        yb = yb * _silu(z_s[:, E_GB + LANES * g:E_GB + LANES * (g + 1)])
        y_s[:, A_WIDTH + LANES * g:A_WIDTH + LANES * (g + 1)] = yb.astype(BF16)

    _memory_attention(z_s, E_QM, E_GM, kt_ref, vm_ref, y_s, A_WIDTH + B_WIDTH)

    o_ref[...] = h_ref[...] + _dot(y_s[...], wout_ref[...])


def _sub_tiles(ref):
    n = ref.shape[0]
    return [(ref.at[k], ref.at[(k - 1) % n]) for k in range(n)]


def _even_kernel(h_ref, g_ref, win_ref, wout_ref, lng_ref, lnb_ref, w2_ref, abias_ref, bw_ref,
                 bscale_ref, kt_ref, vm_ref, o_ref, xn_s, z_s, y_s, sg_s, xb_s):
    t = pl.program_id(1)
    nsub, sub = xn_s.shape[0], xn_s.shape[1]

    @pl.when(t == 0)
    def _():
        xb_s[nsub - 1, sub:sub + POOL_HALO, :] = jnp.zeros((POOL_HALO, B_WIDTH), F32)

    for k, (xb, xb_prev) in enumerate(_sub_tiles(xb_s)):
        rows = pl.ds(k * sub, sub)
        _even_tile(t * (nsub * sub) + k * sub, h_ref.at[rows], g_ref, win_ref, wout_ref, lng_ref, lnb_ref,
                   w2_ref, abias_ref, bw_ref, bscale_ref, kt_ref, vm_ref, o_ref.at[rows],
                   xn_s.at[k], z_s.at[k], y_s.at[k], sg_s.at[k], xb, xb_prev)


def _odd_tile(first, sink_row, h_ref, g_ref, win_ref, wout_ref, qg_ref, kg_ref, sink_ref, dw_ref, dwb_ref, dlg_ref,
              dlb_ref, pw_ref, kt_ref, vm_ref, o_ref, xn_s, z_s, y_s, cv_s, hb_s, kb_s, vb_s,
              hb_prev, kb_prev, vb_prev):
    tile = h_ref.shape[0]
    nb = tile // BLOCK
    hb_s[0, 0:CONV_HALO, :] = hb_prev[0, tile:tile + CONV_HALO, :]
    kb_s[:, 0:BLOCK, :] = kb_prev[:, tile:tile + BLOCK, :]
    vb_s[:, 0:BLOCK, :] = vb_prev[:, tile:tile + BLOCK, :]

    _project_in(h_ref, g_ref, win_ref, xn_s, z_s, ((O_DA, O_QM), (O_QC, O_DA), (O_QM, ODD_IN)))

    lo = _lo_lanes((tile, LANES))
    k = z_s[:, O_KC:O_KC + C_KV_WIDTH]
    kn = k * _pair_rs(k) * (kg_ref[...] * qg_ref[...] * (HEAD_DIM ** -0.5 * LOG2E))
    v = z_s[:, O_VC:O_VC + C_KV_WIDTH]
    for buf, x in ((kb_s, kn), (vb_s, v)):
        xsw = pltpu.roll(x, HEAD_DIM, axis=1)
        buf[0, BLOCK:BLOCK + tile, :] = jnp.where(lo, x, 0.0).astype(BF16)
        buf[1, BLOCK:BLOCK + tile, :] = jnp.where(lo, 0.0, x).astype(BF16)
        buf[2, BLOCK:BLOCK + tile, :] = jnp.where(lo, xsw, 0.0).astype(BF16)
        buf[3, BLOCK:BLOCK + tile, :] = jnp.where(lo, 0.0, xsw).astype(BF16)
    variant = ((0, 3), (2, 1))

    qi = lax.broadcasted_iota(jnp.int32, (BLOCK, 2 * BLOCK), 0)
    cj = lax.broadcasted_iota(jnp.int32, (BLOCK, 2 * BLOCK), 1)
    band = (cj - qi >= 1) & (cj - qi <= BLOCK)
    first_key = first * BLOCK
    neg = jnp.finfo(F32).min
    for j in range(nb):
        valid = band & (cj >= first_key) if j == 0 else band
        for s in range(C_Q_WIDTH // LANES):
            q = z_s[BLOCK * j:BLOCK * (j + 1), O_QC + LANES * s:O_QC + LANES * (s + 1)]
            qn = (q * _pair_rs(q)).astype(BF16)
            acc = None
            for pos in range(2):
                hd = 2 * s + pos
                var = variant[hd // 4][pos]
                sc = _dot_nt(qn, kb_s[var, BLOCK * j:BLOCK * (j + 2), :])
                sc = jnp.where(valid, sc, neg)
                sink = sink_ref[sink_row, hd] * LOG2E
                m = jnp.maximum(jnp.max(sc, axis=-1, keepdims=True), sink)
                e = jnp.exp2(sc - m)
                l = jnp.sum(e, axis=-1, keepdims=True) + jnp.exp2(sink - m)
                o = _dot(e.astype(BF16), vb_s[var, BLOCK * j:BLOCK * (j + 2), :]) * (1.0 / l)
                acc = o if acc is None else acc + o
            gate = _silu(z_s[BLOCK * j:BLOCK * (j + 1), O_GC + LANES * s:O_GC + LANES * (s + 1)])
            y_s[BLOCK * j:BLOCK * (j + 1), LANES * s:LANES * (s + 1)] = (acc * gate).astype(BF16)

    hb_s[0, CONV_HALO:CONV_HALO + tile, :] = (
        z_s[:, O_DA:O_DA + D_WIDTH] * _sigmoid(z_s[:, O_DB:O_DB + D_WIDTH]))
    shifted_rows = CONV_HALO + tile - SUBLANES
    for s in range(D_WIDTH // LANES):
        x = hb_s[0, :, LANES * s:LANES * (s + 1)]
        for p in range(1, SUBLANES):
            hb_s[p, 0:shifted_rows, LANES * s:LANES * (s + 1)] = pltpu.roll(
                x, CONV_HALO + tile - p, axis=0)[0:shifted_rows, :]
    base = CONV_HALO - (CONV_WIDTH - 1)
    for s in range(D_WIDTH // LANES):
        for j in range(nb):
            acc = jnp.broadcast_to(dwb_ref[:, LANES * s:LANES * (s + 1)], (BLOCK, LANES))
            for tap in range(CONV_WIDTH):
                p, r = (base + tap) % SUBLANES, (base + tap) // SUBLANES * SUBLANES
                x = hb_s[p, BLOCK * j + r:BLOCK * (j + 1) + r, LANES * s:LANES * (s + 1)]
                acc = acc + x * dw_ref[tap:tap + 1, LANES * s:LANES * (s + 1)]
            cv_s[BLOCK * j:BLOCK * (j + 1), LANES * s:LANES * (s + 1)] = acc
    hd_ = _silu(_layer_norm(cv_s[...], dlg_ref[...], dlb_ref[...]))
    yd = _dot(hd_.astype(BF16), pw_ref[...]) * _silu(z_s[:, O_GD:O_GD + D_WIDTH])
    y_s[:, C_Q_WIDTH:C_Q_WIDTH + D_WIDTH] = yd.astype(BF16)

    _memory_attention(z_s, O_QM, O_GM, kt_ref, vm_ref, y_s, C_Q_WIDTH + D_WIDTH)

    o_ref[...] = h_ref[...] + _dot(y_s[...], wout_ref[...])


def _odd_kernel(sink_row, h_ref, g_ref, win_ref, wout_ref, qg_ref, kg_ref, sink_ref, dw_ref, dwb_ref, dlg_ref,
                dlb_ref, pw_ref, kt_ref, vm_ref, o_ref, xn_s, z_s, y_s, cv_s, hb_s, kb_s, vb_s):
    t = pl.program_id(1)
    nsub, sub = xn_s.shape[0], xn_s.shape[1]

    @pl.when(t == 0)
    def _():
        hb_s[nsub - 1, 0, sub:sub + CONV_HALO, :] = jnp.zeros((CONV_HALO, D_WIDTH), F32)
        kb_s[nsub - 1, :, sub:sub + BLOCK, :] = jnp.zeros((4, BLOCK, LANES), BF16)
        vb_s[nsub - 1, :, sub:sub + BLOCK, :] = jnp.zeros((4, BLOCK, LANES), BF16)

    halos = zip(_sub_tiles(hb_s), _sub_tiles(kb_s), _sub_tiles(vb_s))
    for k, ((hb, hb_prev), (kb, kb_prev), (vb, vb_prev)) in enumerate(halos):
        rows = pl.ds(k * sub, sub)
        first = (t == 0).astype(jnp.int32) if k == 0 else 0
        _odd_tile(first, sink_row, h_ref.at[rows], g_ref, win_ref, wout_ref, qg_ref, kg_ref, sink_ref, dw_ref,
                  dwb_ref, dlg_ref, dlb_ref, pw_ref, kt_ref, vm_ref, o_ref.at[rows], xn_s.at[k], z_s.at[k],
                  y_s.at[k], cv_s.at[k], hb, kb, vb, hb_prev, kb_prev, vb_prev)


def _mem_prep_kernel(mem_ref, g_ref, w_ref, qn_ref, kn_ref, kt_ref, vm_ref):
    m = mem_ref[...]
    mn = (m * lax.rsqrt(jnp.mean(m * m, axis=-1, keepdims=True) + EPS) * g_ref[...]).astype(BF16)
    zeros = jnp.zeros((HEAD_DIM, N_MEM), F32)
    lo = _lo_lanes((N_MEM, LANES))
    for layer in range(w_ref.shape[0]):
        kv = _dot(mn, w_ref[layer])
        kt = kv[:, 0:M_WIDTH].T
        gain = kn_ref[layer] * qn_ref[layer] * (HEAD_DIM ** -0.5 * LOG2E)
        for hd in range(M_HEADS):
            kh = kt[HEAD_DIM * hd:HEAD_DIM * (hd + 1), :]
            kh = kh * lax.rsqrt(jnp.mean(kh * kh, axis=0, keepdims=True) + EPS) * gain
            parts = [kh, zeros] if hd % 2 == 0 else [zeros, kh]
            kt_ref[layer, hd] = jnp.concatenate(parts, axis=0).astype(BF16)
            s = hd // 2
            v = kv[:, M_WIDTH + LANES * s:M_WIDTH + LANES * (s + 1)]
            vm_ref[layer, hd] = (jnp.where(lo, v, 0.0) if hd % 2 == 0 else jnp.where(lo, 0.0, v)).astype(BF16)


def _mem_prep(mem, mem_norm_g, w_mem_kv, m_qnorm, m_knorm):
    depth, batch = w_mem_kv.shape[0], mem.shape[0]
    whole = lambda shape: pl.BlockSpec(shape, lambda b: (0,) * len(shape))
    return pl.pallas_call(
        _mem_prep_kernel,
        grid=(batch,),
        in_specs=[
            pl.BlockSpec((None, N_MEM, D_MODEL), lambda b: (b, 0, 0)),
            whole((1, D_MODEL)),
            whole((depth, D_MODEL, 2 * M_WIDTH)),
            whole((depth, HEAD_DIM, 1)),
            whole((depth, HEAD_DIM, 1)),
        ],
        out_specs=[
            pl.BlockSpec((depth, None, M_HEADS, LANES, N_MEM), lambda b: (0, b, 0, 0, 0)),
            pl.BlockSpec((depth, None, M_HEADS, N_MEM, LANES), lambda b: (0, b, 0, 0, 0)),
        ],
        out_shape=[
            jax.ShapeDtypeStruct((depth, batch, M_HEADS, LANES, N_MEM), BF16),
            jax.ShapeDtypeStruct((depth, batch, M_HEADS, N_MEM, LANES), BF16),
        ],
        compiler_params=pltpu.CompilerParams(dimension_semantics=("arbitrary",)),
        name="mem_prep",
    )(mem, mem_norm_g.reshape(1, D_MODEL), w_mem_kv.astype(BF16),
      m_qnorm.reshape(depth, HEAD_DIM, 1), m_knorm.reshape(depth, HEAD_DIM, 1))


def _layer_of(stacked, index):
    tail = stacked.shape[1:]
    return pl.BlockSpec((None,) + tail, lambda b, t: (index,) + (0,) * len(tail))


def _rows(x):
    return x.reshape(x.shape[0], 1, x.shape[1])


def _layer_call(body, name, h, in_width, nsub, scratch, operands, layer, kt, vm):
    batch, seq, _ = h.shape
    sub = TILE // nsub
    h_spec = pl.BlockSpec((None, TILE, D_MODEL), lambda b, t: (b, t, 0))
    mem_specs = [
        pl.BlockSpec((None, None, M_HEADS, LANES, N_MEM), lambda b, t: (layer, b, 0, 0, 0)),
        pl.BlockSpec((None, None, M_HEADS, N_MEM, LANES), lambda b, t: (layer, b, 0, 0, 0)),
    ]
    return pl.pallas_call(
        body,
        grid=(batch, seq // TILE),
        in_specs=[h_spec] + [spec for _, spec in operands] + mem_specs,
        out_specs=h_spec,
        out_shape=jax.ShapeDtypeStruct(h.shape, h.dtype),
        scratch_shapes=[
            pltpu.VMEM((nsub, sub, D_MODEL), BF16),
            pltpu.VMEM((nsub, sub, in_width), F32),
            pltpu.VMEM((nsub, sub, OUT_WIDTH), BF16),
        ] + scratch,
        compiler_params=pltpu.CompilerParams(
            dimension_semantics=("arbitrary", "arbitrary"), vmem_limit_bytes=VMEM_LIMIT),
        name=name,
    )(h, *[x for x, _ in operands], kt, vm)


def kernel(x, mem, norm_g, mem_norm_g, w_mem_kv, m_qnorm, m_knorm, w_in_even, w_out_even, a_ln_g, a_ln_b, a_ws, a_bs, b_w, b_scale, w_in_odd, w_out_odd, c_qnorm, c_knorm, c_sink, d_dw, d_dw_b, d_ln_g, d_ln_b, d_pw):
    kt, vm = _mem_prep(mem, mem_norm_g, w_mem_kv, m_qnorm, m_knorm)

    n_even, heads = a_ws.shape[0], a_ws.shape[1]
    norm_rows = _rows(norm_g)
    even = dict(
        w_in=w_in_even.astype(BF16), w_out=w_out_even.astype(BF16), ln_g=_rows(a_ln_g), ln_b=_rows(a_ln_b),
        w2=a_ws.reshape(n_even, heads // 2, 2, CHUNK, CHUNK).transpose(0, 1, 3, 2, 4)
               .reshape(n_even, heads // 2, CHUNK, 2 * CHUNK).astype(BF16),
        abias=jnp.repeat(a_bs.reshape(n_even, heads // 2, 2, CHUNK).transpose(0, 1, 3, 2), HEAD_DIM, axis=3),
        bw=b_w.astype(BF16), bscale=_rows(b_scale))
    odd = dict(
        w_in=w_in_odd.astype(BF16), w_out=w_out_odd.astype(BF16),
        qg=_rows(jnp.tile(c_qnorm, (1, 2))), kg=_rows(jnp.tile(c_knorm, (1, 2))),
        dw=d_dw, dwb=_rows(d_dw_b), ln_g=_rows(d_ln_g), ln_b=_rows(d_ln_b), pw=d_pw.astype(BF16))
    even_scratch = [pltpu.VMEM((EVEN_SUB_TILES, TILE // EVEN_SUB_TILES, A_WIDTH), F32),
                    pltpu.VMEM((EVEN_SUB_TILES, POOL_HALO + TILE // EVEN_SUB_TILES, B_WIDTH), F32)]
    odd_sub = TILE // ODD_SUB_TILES
    odd_scratch = [pltpu.VMEM((ODD_SUB_TILES, odd_sub, D_WIDTH), F32),
                   pltpu.VMEM((ODD_SUB_TILES, SUBLANES, CONV_HALO + odd_sub, D_WIDTH), F32),
                   pltpu.VMEM((ODD_SUB_TILES, 4, BLOCK + odd_sub, LANES), BF16),
                   pltpu.VMEM((ODD_SUB_TILES, 4, BLOCK + odd_sub, LANES), BF16)]

    h = x
    for layer in range(w_mem_kv.shape[0]):
        i = layer // 2
        norm = (norm_rows, _layer_of(norm_rows, layer))
        if layer % 2 == 0:
            names = ("w_in", "w_out", "ln_g", "ln_b", "w2", "abias", "bw", "bscale")
            operands = [norm] + [(even[n], _layer_of(even[n], i)) for n in names]
            h = _layer_call(_even_kernel, "even_layer", h, EVEN_IN, EVEN_SUB_TILES, even_scratch, operands,
                            layer, kt, vm)
        else:
            operands = [norm] + [(odd[n], _layer_of(odd[n], i)) for n in ("w_in", "w_out", "qg", "kg")]
            operands.append((c_sink, pl.BlockSpec(memory_space=pltpu.SMEM)))
            operands += [(odd[n], _layer_of(odd[n], i)) for n in ("dw", "dwb", "ln_g", "ln_b", "pw")]
            h = _layer_call(functools.partial(_odd_kernel, i), "odd_layer", h, ODD_IN, ODD_SUB_TILES,
                            odd_scratch, operands, layer, kt, vm)
    return h
```

```python
import functools

import jax
import jax.numpy as jnp
from jax import lax
from jax.experimental import pallas as pl
from jax.experimental.pallas import tpu as pltpu

D_MODEL = 1024
N_MEM = 256
HEAD_DIM = 64
EPS = 1e-6
LOG2E = 1.4426950408889634
A_WIDTH = 512
CHUNK = 128
POOL_WINDOWS = (2, 4, 8, 16)
B_WIDTH = 512
C_Q_WIDTH = 512
C_KV_WIDTH = 128
BLOCK = 128
D_WIDTH = 512
CONV_WIDTH = 31
M_HEADS = 4
M_WIDTH = 256
OUT_WIDTH = 1280
EVEN_IN = 3072
ODD_IN = 3328

LANES = 128
SUBLANES = 8
POOL_HALO = 16
CONV_HALO = 32
TILE = 512
EVEN_SUB_TILES = 2
ODD_SUB_TILES = 1
VMEM_LIMIT = 56 * 1024 * 1024

F32 = jnp.float32
BF16 = jnp.bfloat16

E_U, E_V, E_GA, E_XB, E_GB, E_QM, E_GM = 0, 512, 1024, 1536, 2048, 2560, 2816
O_QC, O_KC, O_VC, O_GC, O_DA, O_DB, O_GD, O_QM, O_GM = 0, 512, 640, 768, 1280, 1792, 2304, 2816, 3072


def _dot(a, b):
    return jnp.dot(a, b, preferred_element_type=F32)


def _dot_nt(a, b):
    return lax.dot_general(a, b, (((1,), (1,)), ((), ())), preferred_element_type=F32)


def _sigmoid(x):
    return 0.5 * jnp.tanh(0.5 * x) + 0.5


def _silu(x):
    hx = 0.5 * x
    return hx * jnp.tanh(hx) + hx


def _lo_lanes(shape):
    return lax.broadcasted_iota(jnp.int32, shape, len(shape) - 1) < HEAD_DIM


def _pair_rs(x):
    lo = _lo_lanes(x.shape)
    x2 = x * x
    s_lo = jnp.sum(jnp.where(lo, x2, 0.0), axis=-1, keepdims=True)
    s_hi = jnp.sum(jnp.where(lo, 0.0, x2), axis=-1, keepdims=True)
    return lax.rsqrt(jnp.where(lo, s_lo, s_hi) * (1.0 / HEAD_DIM) + EPS)


def _layer_norm(x, g, b):
    mu = jnp.mean(x, axis=-1, keepdims=True)
    xc = x - mu
    return xc * lax.rsqrt(jnp.mean(xc * xc, axis=-1, keepdims=True) + EPS) * g + b


def _project_in(h_ref, g_ref, win_ref, xn_s, z_s, groups):
    h = h_ref[...]
    xn = h * lax.rsqrt(jnp.mean(h * h, axis=-1, keepdims=True) + EPS) * g_ref[...]
    xn_s[...] = xn.astype(BF16)
    for lo, hi in groups:
        z_s[:, lo:hi] = _dot(xn_s[...], win_ref[:, lo:hi])


def _memory_attention(z_s, q_off, g_off, kt_ref, vm_ref, y_s, y_off, rows=slice(None)):
    for s in range(M_WIDTH // LANES):
        q = z_s[rows, q_off + LANES * s:q_off + LANES * (s + 1)]
        qn = (q * _pair_rs(q)).astype(BF16)
        acc = None
        for pos in range(2):
            hd = 2 * s + pos
            sc = _dot(qn, kt_ref[hd])
            m = jnp.max(sc, axis=-1, keepdims=True)
            e = jnp.exp2(sc - m)
            l = jnp.sum(e, axis=-1, keepdims=True)
            o = _dot(e.astype(BF16), vm_ref[hd]) * (1.0 / l)
            acc = o if acc is None else acc + o
        gate = _silu(z_s[rows, g_off + LANES * s:g_off + LANES * (s + 1)])
        y_s[rows, y_off + LANES * s:y_off + LANES * (s + 1)] = (acc * gate).astype(BF16)


def _even_tile(t_abs0, h_ref, g_ref, win_ref, wout_ref, lng_ref, lnb_ref, w2_ref, abias_ref, bw_ref,
               bscale_ref, kt_ref, vm_ref, o_ref, xn_s, z_s, y_s, sg_s, xb_s, xb_prev):
    tile = h_ref.shape[0]
    nc = tile // CHUNK
    xb_s[0:POOL_HALO, :] = xb_prev[tile:tile + POOL_HALO, :]

    _project_in(h_ref, g_ref, win_ref, xn_s, z_s, ((E_U, E_XB), (E_XB, E_QM), (E_QM, EVEN_IN)))

    vn = _layer_norm(z_s[:, E_V:E_V + A_WIDTH], lng_ref[...], lnb_ref[...])
    lo = _lo_lanes((CHUNK, LANES))
    row = lax.broadcasted_iota(jnp.int32, (CHUNK, 2 * CHUNK), 0)
    col = lax.broadcasted_iota(jnp.int32, (CHUNK, 2 * CHUNK), 1)
    tril2 = (col & (CHUNK - 1)) <= row
    for s in range(A_WIDTH // LANES):
        tops, bots = [], []
        for c in range(nc):
            blk = vn[CHUNK * c:CHUNK * (c + 1), LANES * s:LANES * (s + 1)]
            tops.append(jnp.where(lo, blk, 0.0).astype(BF16))
            bots.append(jnp.where(lo, 0.0, blk).astype(BF16))
        rhs = jnp.concatenate([jnp.concatenate(tops, axis=1), jnp.concatenate(bots, axis=1)], axis=0)
        w2 = jnp.where(tril2, w2_ref[s], jnp.zeros((), BF16))
        sg = _dot(w2, rhs)
        for c in range(nc):
            sg_s[CHUNK * c:CHUNK * (c + 1), LANES * s:LANES * (s + 1)] = (
                sg[:, LANES * c:LANES * (c + 1)] + abias_ref[s])
    ya = z_s[:, E_U:E_U + A_WIDTH] * sg_s[...] * _silu(z_s[:, E_GA:E_GA + A_WIDTH])
    y_s[:, 0:A_WIDTH] = ya.astype(BF16)

    xb_s[POOL_HALO:POOL_HALO + tile, :] = z_s[:, E_XB:E_XB + B_WIDTH]
    t_abs = t_abs0 + lax.broadcasted_iota(jnp.int32, (tile, 1), 0)
    for g, w in enumerate(POOL_WINDOWS):
        win = xb_s[:, LANES * g:LANES * (g + 1)]
        acc = win
        k = 1
        while k < w:
            acc = acc + pltpu.roll(acc, k, axis=0)
            k *= 2
        cnt = jnp.minimum(t_abs + 1, w).astype(F32)
        pooled = acc[POOL_HALO:, :] * (1.0 / cnt)
        diff = pooled - win[POOL_HALO:, :]
        yb = _dot(diff.astype(BF16), bw_ref[g]) * bscale_ref[:, LANES * g:LANES * (g + 1)]
        yb = yb * _silu(z_s[:, E_GB + LANES * g:E_GB + LANES * (g + 1)])
        y_s[:, A_WIDTH + LANES * g:A_WIDTH + LANES * (g + 1)] = yb.astype(BF16)

    _memory_attention(z_s, E_QM, E_GM, kt_ref, vm_ref, y_s, A_WIDTH + B_WIDTH)

    o_ref[...] = h_ref[...] + _dot(y_s[...], wout_ref[...])


def _sub_tiles(ref):
    n = ref.shape[0]
    return [(ref.at[k], ref.at[(k - 1) % n]) for k in range(n)]


def _even_kernel(h_ref, g_ref, win_ref, wout_ref, lng_ref, lnb_ref, w2_ref, abias_ref, bw_ref,
                 bscale_ref, kt_ref, vm_ref, o_ref, xn_s, z_s, y_s, sg_s, xb_s):
    t = pl.program_id(1)
    nsub, sub = xn_s.shape[0], xn_s.shape[1]

    @pl.when(t == 0)
    def _():
        xb_s[nsub - 1, sub:sub + POOL_HALO, :] = jnp.zeros((POOL_HALO, B_WIDTH), F32)

    for k, (xb, xb_prev) in enumerate(_sub_tiles(xb_s)):
        rows = pl.ds(k * sub, sub)
        _even_tile(t * (nsub * sub) + k * sub, h_ref.at[rows], g_ref, win_ref, wout_ref, lng_ref, lnb_ref,
                   w2_ref, abias_ref, bw_ref, bscale_ref, kt_ref, vm_ref, o_ref.at[rows],
                   xn_s.at[k], z_s.at[k], y_s.at[k], sg_s.at[k], xb, xb_prev)


def _odd_tile(first, sink_row, h_ref, g_ref, win_ref, wout_ref, qg_ref, kg_ref, sink_ref, dw_ref, dwb_ref, dlg_ref,
              dlb_ref, pw_ref, kt_ref, vm_ref, o_ref, xn_s, z_s, y_s, cv_s, hb_s, kb_s, vb_s,
              hb_prev, kb_prev, vb_prev):
    tile = h_ref.shape[0]
    nb = tile // BLOCK
    hb_s[0, 0:CONV_HALO, :] = hb_prev[0, tile:tile + CONV_HALO, :]
    kb_s[:, 0:BLOCK, :] = kb_prev[:, tile:tile + BLOCK, :]
    vb_s[:, 0:BLOCK, :] = vb_prev[:, tile:tile + BLOCK, :]

    halves = [slice(r, r + tile // 2) for r in (0, tile // 2)]
    for rows in halves:
        h = h_ref[rows, :]
        xn_s[rows, :] = (h * lax.rsqrt(jnp.mean(h * h, axis=-1, keepdims=True) + EPS) * g_ref[...]).astype(BF16)
        z_s[rows, O_DA:O_GD] = _dot(xn_s[rows, :], win_ref[:, O_DA:O_GD])
    for c0, c1 in ((O_GD, O_QM), (O_KC, O_GC), (O_QC, O_KC), (O_GC, O_DA), (O_QM, ODD_IN)):
        for rows in halves:
            z_s[rows, c0:c1] = _dot(xn_s[rows, :], win_ref[:, c0:c1])

    lo = _lo_lanes((tile, LANES))
    k = z_s[:, O_KC:O_KC + C_KV_WIDTH]
    kn = k * _pair_rs(k) * (kg_ref[...] * qg_ref[...] * (HEAD_DIM ** -0.5 * LOG2E))
    v = z_s[:, O_VC:O_VC + C_KV_WIDTH]
    for buf, x in ((kb_s, kn), (vb_s, v)):
        xsw = pltpu.roll(x, HEAD_DIM, axis=1)
        buf[0, BLOCK:BLOCK + tile, :] = jnp.where(lo, x, 0.0).astype(BF16)
        buf[1, BLOCK:BLOCK + tile, :] = jnp.where(lo, 0.0, x).astype(BF16)
        buf[2, BLOCK:BLOCK + tile, :] = jnp.where(lo, xsw, 0.0).astype(BF16)
        buf[3, BLOCK:BLOCK + tile, :] = jnp.where(lo, 0.0, xsw).astype(BF16)
    variant = ((0, 3), (2, 1))

    qi = lax.broadcasted_iota(jnp.int32, (BLOCK, 2 * BLOCK), 0)
    cj = lax.broadcasted_iota(jnp.int32, (BLOCK, 2 * BLOCK), 1)
    band = (cj - qi >= 1) & (cj - qi <= BLOCK)
    first_key = first * BLOCK
    neg = jnp.finfo(F32).min
    for j in range(nb):
        valid = band & (cj >= first_key) if j == 0 else band
        for s in range(C_Q_WIDTH // LANES):
            q = z_s[BLOCK * j:BLOCK * (j + 1), O_QC + LANES * s:O_QC + LANES * (s + 1)]
            qn = (q * _pair_rs(q)).astype(BF16)
            acc = None
            for pos in range(2):
                hd = 2 * s + pos
                var = variant[hd // 4][pos]
                sc = _dot_nt(qn, kb_s[var, BLOCK * j:BLOCK * (j + 2), :])
                sc = jnp.where(valid, sc, neg)
                sink = sink_ref[sink_row, hd] * LOG2E
                m = jnp.maximum(jnp.max(sc, axis=-1, keepdims=True), sink)
                e = jnp.exp2(sc - m)
                l = jnp.sum(e, axis=-1, keepdims=True) + jnp.exp2(sink - m)
                o = _dot(e.astype(BF16), vb_s[var, BLOCK * j:BLOCK * (j + 2), :]) * (1.0 / l)
                acc = o if acc is None else acc + o
            gate = _silu(z_s[BLOCK * j:BLOCK * (j + 1), O_GC + LANES * s:O_GC + LANES * (s + 1)])
            y_s[BLOCK * j:BLOCK * (j + 1), LANES * s:LANES * (s + 1)] = (acc * gate).astype(BF16)

    for rows in halves:
        hb_s[0, CONV_HALO + rows.start:CONV_HALO + rows.stop, :] = (
            z_s[rows, O_DA:O_DA + D_WIDTH] * _sigmoid(z_s[rows, O_DB:O_DB + D_WIDTH]))
    shifted_rows = CONV_HALO + tile - SUBLANES
    for s in range(D_WIDTH // LANES):
        x = hb_s[0, :, LANES * s:LANES * (s + 1)]
        for p in range(1, SUBLANES):
            hb_s[p, 0:shifted_rows, LANES * s:LANES * (s + 1)] = pltpu.roll(
                x, CONV_HALO + tile - p, axis=0)[0:shifted_rows, :]
    base = CONV_HALO - (CONV_WIDTH - 1)
    for s in range(D_WIDTH // LANES):
        for j in range(nb):
            acc = jnp.broadcast_to(dwb_ref[:, LANES * s:LANES * (s + 1)], (BLOCK, LANES))
            for tap in range(CONV_WIDTH):
                p, r = (base + tap) % SUBLANES, (base + tap) // SUBLANES * SUBLANES
                x = hb_s[p, BLOCK * j + r:BLOCK * (j + 1) + r, LANES * s:LANES * (s + 1)]
                acc = acc + x * dw_ref[tap:tap + 1, LANES * s:LANES * (s + 1)]
            cv_s[BLOCK * j:BLOCK * (j + 1), LANES * s:LANES * (s + 1)] = acc
    for rows in halves:
        hd_ = _silu(_layer_norm(cv_s[rows, :], dlg_ref[...], dlb_ref[...]))
        yd = _dot(hd_.astype(BF16), pw_ref[...]) * _silu(z_s[rows, O_GD:O_GD + D_WIDTH])
        y_s[rows, C_Q_WIDTH:C_Q_WIDTH + D_WIDTH] = yd.astype(BF16)

    for rows in halves:
        _memory_attention(z_s, O_QM, O_GM, kt_ref, vm_ref, y_s, C_Q_WIDTH + D_WIDTH, rows)
    for rows in halves:
        o_ref[rows, :] = h_ref[rows, :] + _dot(y_s[rows, :], wout_ref[...])


def _odd_kernel(sink_row, h_ref, g_ref, win_ref, wout_ref, qg_ref, kg_ref, sink_ref, dw_ref, dwb_ref, dlg_ref,
                dlb_ref, pw_ref, kt_ref, vm_ref, o_ref, xn_s, z_s, y_s, cv_s, hb_s, kb_s, vb_s):
    t = pl.program_id(1)
    nsub, sub = xn_s.shape[0], xn_s.shape[1]

    @pl.when(t == 0)
    def _():
        hb_s[nsub - 1, 0, sub:sub + CONV_HALO, :] = jnp.zeros((CONV_HALO, D_WIDTH), F32)
        kb_s[nsub - 1, :, sub:sub + BLOCK, :] = jnp.zeros((4, BLOCK, LANES), BF16)
        vb_s[nsub - 1, :, sub:sub + BLOCK, :] = jnp.zeros((4, BLOCK, LANES), BF16)

    halos = zip(_sub_tiles(hb_s), _sub_tiles(kb_s), _sub_tiles(vb_s))
    for k, ((hb, hb_prev), (kb, kb_prev), (vb, vb_prev)) in enumerate(halos):
        rows = pl.ds(k * sub, sub)
        first = (t == 0).astype(jnp.int32) if k == 0 else 0
        _odd_tile(first, sink_row, h_ref.at[rows], g_ref, win_ref, wout_ref, qg_ref, kg_ref, sink_ref, dw_ref,
                  dwb_ref, dlg_ref, dlb_ref, pw_ref, kt_ref, vm_ref, o_ref.at[rows], xn_s.at[k], z_s.at[k],
                  y_s.at[k], cv_s.at[k], hb, kb, vb, hb_prev, kb_prev, vb_prev)


def _mem_prep_kernel(mem_ref, g_ref, w_ref, qn_ref, kn_ref, kt_ref, vm_ref):
    m = mem_ref[...]
    mn = (m * lax.rsqrt(jnp.mean(m * m, axis=-1, keepdims=True) + EPS) * g_ref[...]).astype(BF16)
    zeros = jnp.zeros((HEAD_DIM, N_MEM), F32)
    lo = _lo_lanes((N_MEM, LANES))
    for layer in range(w_ref.shape[0]):
        kv = _dot(mn, w_ref[layer])
        kt = kv[:, 0:M_WIDTH].T
        gain = kn_ref[layer] * qn_ref[layer] * (HEAD_DIM ** -0.5 * LOG2E)
        for hd in range(M_HEADS):
            kh = kt[HEAD_DIM * hd:HEAD_DIM * (hd + 1), :]
            kh = kh * lax.rsqrt(jnp.mean(kh * kh, axis=0, keepdims=True) + EPS) * gain
            parts = [kh, zeros] if hd % 2 == 0 else [zeros, kh]
            kt_ref[layer, hd] = jnp.concatenate(parts, axis=0).astype(BF16)
            s = hd // 2
            v = kv[:, M_WIDTH + LANES * s:M_WIDTH + LANES * (s + 1)]
            vm_ref[layer, hd] = (jnp.where(lo, v, 0.0) if hd % 2 == 0 else jnp.where(lo, 0.0, v)).astype(BF16)


def _mem_prep(mem, mem_norm_g, w_mem_kv, m_qnorm, m_knorm):
    depth, batch = w_mem_kv.shape[0], mem.shape[0]
    whole = lambda shape: pl.BlockSpec(shape, lambda b: (0,) * len(shape))
    return pl.pallas_call(
        _mem_prep_kernel,
        grid=(batch,),
        in_specs=[
            pl.BlockSpec((None, N_MEM, D_MODEL), lambda b: (b, 0, 0)),
            whole((1, D_MODEL)),
            whole((depth, D_MODEL, 2 * M_WIDTH)),
            whole((depth, HEAD_DIM, 1)),
            whole((depth, HEAD_DIM, 1)),
        ],
        out_specs=[
            pl.BlockSpec((depth, None, M_HEADS, LANES, N_MEM), lambda b: (0, b, 0, 0, 0)),
            pl.BlockSpec((depth, None, M_HEADS, N_MEM, LANES), lambda b: (0, b, 0, 0, 0)),
        ],
        out_shape=[
            jax.ShapeDtypeStruct((depth, batch, M_HEADS, LANES, N_MEM), BF16),
            jax.ShapeDtypeStruct((depth, batch, M_HEADS, N_MEM, LANES), BF16),
        ],
        compiler_params=pltpu.CompilerParams(dimension_semantics=("arbitrary",)),
        name="mem_prep",
    )(mem, mem_norm_g.reshape(1, D_MODEL), w_mem_kv.astype(BF16),
      m_qnorm.reshape(depth, HEAD_DIM, 1), m_knorm.reshape(depth, HEAD_DIM, 1))


def _layer_of(stacked, index):
    tail = stacked.shape[1:]
    return pl.BlockSpec((None,) + tail, lambda b, t: (index,) + (0,) * len(tail))


def _rows(x):
    return x.reshape(x.shape[0], 1, x.shape[1])


def _layer_call(body, name, h, in_width, nsub, scratch, operands, layer, kt, vm):
    batch, seq, _ = h.shape
    sub = TILE // nsub
    h_spec = pl.BlockSpec((None, TILE, D_MODEL), lambda b, t: (b, t, 0))
    mem_specs = [
        pl.BlockSpec((None, None, M_HEADS, LANES, N_MEM), lambda b, t: (layer, b, 0, 0, 0)),
        pl.BlockSpec((None, None, M_HEADS, N_MEM, LANES), lambda b, t: (layer, b, 0, 0, 0)),
    ]
    return pl.pallas_call(
        body,
        grid=(batch, seq // TILE),
        in_specs=[h_spec] + [spec for _, spec in operands] + mem_specs,
        out_specs=h_spec,
        out_shape=jax.ShapeDtypeStruct(h.shape, h.dtype),
        scratch_shapes=[
            pltpu.VMEM((nsub, sub, D_MODEL), BF16),
            pltpu.VMEM((nsub, sub, in_width), F32),
            pltpu.VMEM((nsub, sub, OUT_WIDTH), BF16),
        ] + scratch,
        compiler_params=pltpu.CompilerParams(
            dimension_semantics=("arbitrary", "arbitrary"), vmem_limit_bytes=VMEM_LIMIT),
        name=name,
    )(h, *[x for x, _ in operands], kt, vm)


def kernel(x, mem, norm_g, mem_norm_g, w_mem_kv, m_qnorm, m_knorm, w_in_even, w_out_even, a_ln_g, a_ln_b, a_ws, a_bs, b_w, b_scale, w_in_odd, w_out_odd, c_qnorm, c_knorm, c_sink, d_dw, d_dw_b, d_ln_g, d_ln_b, d_pw):
    kt, vm = _mem_prep(mem, mem_norm_g, w_mem_kv, m_qnorm, m_knorm)

    n_even, heads = a_ws.shape[0], a_ws.shape[1]
    norm_rows = _rows(norm_g)
    even = dict(
        w_in=w_in_even.astype(BF16), w_out=w_out_even.astype(BF16), ln_g=_rows(a_ln_g), ln_b=_rows(a_ln_b),
        w2=a_ws.reshape(n_even, heads // 2, 2, CHUNK, CHUNK).transpose(0, 1, 3, 2, 4)
               .reshape(n_even, heads // 2, CHUNK, 2 * CHUNK).astype(BF16),
        abias=jnp.repeat(a_bs.reshape(n_even, heads // 2, 2, CHUNK).transpose(0, 1, 3, 2), HEAD_DIM, axis=3),
        bw=b_w.astype(BF16), bscale=_rows(b_scale))
    odd = dict(
        w_in=w_in_odd.astype(BF16), w_out=w_out_odd.astype(BF16),
        qg=_rows(jnp.tile(c_qnorm, (1, 2))), kg=_rows(jnp.tile(c_knorm, (1, 2))),
        dw=d_dw, dwb=_rows(d_dw_b), ln_g=_rows(d_ln_g), ln_b=_rows(d_ln_b), pw=d_pw.astype(BF16))
    even_scratch = [pltpu.VMEM((EVEN_SUB_TILES, TILE // EVEN_SUB_TILES, A_WIDTH), F32),
                    pltpu.VMEM((EVEN_SUB_TILES, POOL_HALO + TILE // EVEN_SUB_TILES, B_WIDTH), F32)]
    odd_sub = TILE // ODD_SUB_TILES
    odd_scratch = [pltpu.VMEM((ODD_SUB_TILES, odd_sub, D_WIDTH), F32),
                   pltpu.VMEM((ODD_SUB_TILES, SUBLANES, CONV_HALO + odd_sub, D_WIDTH), F32),
                   pltpu.VMEM((ODD_SUB_TILES, 4, BLOCK + odd_sub, LANES), BF16),
                   pltpu.VMEM((ODD_SUB_TILES, 4, BLOCK + odd_sub, LANES), BF16)]

    h = x
    for layer in range(w_mem_kv.shape[0]):
        i = layer // 2
        norm = (norm_rows, _layer_of(norm_rows, layer))
        if layer % 2 == 0:
            names = ("w_in", "w_out", "ln_g", "ln_b", "w2", "abias", "bw", "bscale")
            operands = [norm] + [(even[n], _layer_of(even[n], i)) for n in names]
            h = _layer_call(_even_kernel, "even_layer", h, EVEN_IN, EVEN_SUB_TILES, even_scratch, operands,
                            layer, kt, vm)
        else:
            operands = [norm] + [(odd[n], _layer_of(odd[n], i)) for n in ("w_in", "w_out", "qg", "kg")]
            operands.append((c_sink, pl.BlockSpec(memory_space=pltpu.SMEM)))
            operands += [(odd[n], _layer_of(odd[n], i)) for n in ("dw", "dwb", "ln_g", "ln_b", "pw")]
            h = _layer_call(functools.partial(_odd_kernel, i), "odd_layer", h, ODD_IN, ODD_SUB_TILES,
                            odd_scratch, operands, layer, kt, vm)
    return h
```

```python
import functools

import jax
import jax.numpy as jnp
from jax import lax
from jax.experimental import pallas as pl
from jax.experimental.pallas import tpu as pltpu

D_MODEL = 1024
N_MEM = 256
HEAD_DIM = 64
EPS = 1e-6
LOG2E = 1.4426950408889634
A_WIDTH = 512
CHUNK = 128
POOL_WINDOWS = (2, 4, 8, 16)
B_WIDTH = 512
C_Q_WIDTH = 512
C_KV_WIDTH = 128
BLOCK = 128
D_WIDTH = 512
CONV_WIDTH = 31
M_HEADS = 4
M_WIDTH = 256
OUT_WIDTH = 1280
EVEN_IN = 3072
ODD_IN = 3328

LANES = 128
SUBLANES = 8
POOL_HALO = 16
CONV_HALO = 32
TILE = 512
EVEN_TILE = 1024
EVEN_SUB_TILES = 4
ODD_SUB_TILES = 1
VMEM_LIMIT = 56 * 1024 * 1024

F32 = jnp.float32
BF16 = jnp.bfloat16

E_U, E_V, E_GA, E_XB, E_GB, E_QM, E_GM = 0, 512, 1024, 1536, 2048, 2560, 2816
O_QC, O_KC, O_VC, O_GC, O_DA, O_DB, O_GD, O_QM, O_GM = 0, 512, 640, 768, 1280, 1792, 2304, 2816, 3072


def _dot(a, b):
    return jnp.dot(a, b, preferred_element_type=F32)


def _dot_nt(a, b):
    return lax.dot_general(a, b, (((1,), (1,)), ((), ())), preferred_element_type=F32)


def _sigmoid(x):
    return 0.5 * jnp.tanh(0.5 * x) + 0.5


def _silu(x):
    hx = 0.5 * x
    return hx * jnp.tanh(hx) + hx


def _lo_lanes(shape):
    return lax.broadcasted_iota(jnp.int32, shape, len(shape) - 1) < HEAD_DIM


def _pair_rs(x):
    lo = _lo_lanes(x.shape)
    x2 = x * x
    s_lo = jnp.sum(jnp.where(lo, x2, 0.0), axis=-1, keepdims=True)
    s_hi = jnp.sum(jnp.where(lo, 0.0, x2), axis=-1, keepdims=True)
    return lax.rsqrt(jnp.where(lo, s_lo, s_hi) * (1.0 / HEAD_DIM) + EPS)


def _layer_norm(x, g, b):
    mu = jnp.mean(x, axis=-1, keepdims=True)
    xc = x - mu
    return xc * lax.rsqrt(jnp.mean(xc * xc, axis=-1, keepdims=True) + EPS) * g + b


def _project_in(h_ref, g_ref, win_ref, xn_s, z_s, groups):
    h = h_ref[...]
    xn = h * lax.rsqrt(jnp.mean(h * h, axis=-1, keepdims=True) + EPS) * g_ref[...]
    xn_s[...] = xn.astype(BF16)
    for lo, hi in groups:
        z_s[:, lo:hi] = _dot(xn_s[...], win_ref[:, lo:hi])


def _memory_attention(z_s, q_off, g_off, kt_ref, vm_ref, y_s, y_off, rows=slice(None)):
    for s in range(M_WIDTH // LANES):
        q = z_s[rows, q_off + LANES * s:q_off + LANES * (s + 1)]
        qn = (q * _pair_rs(q)).astype(BF16)
        acc = None
        for pos in range(2):
            hd = 2 * s + pos
            sc = _dot(qn, kt_ref[hd])
            m = jnp.max(sc, axis=-1, keepdims=True)
            e = jnp.exp2(sc - m)
            l = jnp.sum(e, axis=-1, keepdims=True)
            o = _dot(e.astype(BF16), vm_ref[hd]) * (1.0 / l)
            acc = o if acc is None else acc + o
        gate = _silu(z_s[rows, g_off + LANES * s:g_off + LANES * (s + 1)])
        y_s[rows, y_off + LANES * s:y_off + LANES * (s + 1)] = (acc * gate).astype(BF16)


def _even_tile(t_abs0, h_ref, g_ref, win_ref, wout_ref, lng_ref, lnb_ref, w2_ref, abias_ref, bw_ref,
               bscale_ref, kt_ref, vm_ref, o_ref, xn_s, z_s, y_s, sg_s, xb_s, xb_prev):
    tile = h_ref.shape[0]
    nc = tile // CHUNK
    xb_s[0:POOL_HALO, :] = xb_prev[tile:tile + POOL_HALO, :]

    _project_in(h_ref, g_ref, win_ref, xn_s, z_s, ((E_U, E_XB), (E_XB, E_QM), (E_QM, EVEN_IN)))

    vn = _layer_norm(z_s[:, E_V:E_V + A_WIDTH], lng_ref[...], lnb_ref[...])
    lo = _lo_lanes((CHUNK, LANES))
    row = lax.broadcasted_iota(jnp.int32, (CHUNK, 2 * CHUNK), 0)
    col = lax.broadcasted_iota(jnp.int32, (CHUNK, 2 * CHUNK), 1)
    tril2 = (col & (CHUNK - 1)) <= row
    for s in range(A_WIDTH // LANES):
        tops, bots = [], []
        for c in range(nc):
            blk = vn[CHUNK * c:CHUNK * (c + 1), LANES * s:LANES * (s + 1)]
            tops.append(jnp.where(lo, blk, 0.0).astype(BF16))
            bots.append(jnp.where(lo, 0.0, blk).astype(BF16))
        rhs = jnp.concatenate([jnp.concatenate(tops, axis=1), jnp.concatenate(bots, axis=1)], axis=0)
        w2 = jnp.where(tril2, w2_ref[s], jnp.zeros((), BF16))
        sg = _dot(w2, rhs)
        for c in range(nc):
            sg_s[CHUNK * c:CHUNK * (c + 1), LANES * s:LANES * (s + 1)] = (
                sg[:, LANES * c:LANES * (c + 1)] + abias_ref[s])
    ya = z_s[:, E_U:E_U + A_WIDTH] * sg_s[...] * _silu(z_s[:, E_GA:E_GA + A_WIDTH])
    y_s[:, 0:A_WIDTH] = ya.astype(BF16)

    xb_s[POOL_HALO:POOL_HALO + tile, :] = z_s[:, E_XB:E_XB + B_WIDTH]
    t_abs = t_abs0 + lax.broadcasted_iota(jnp.int32, (tile, 1), 0)
    for g, w in enumerate(POOL_WINDOWS):
        win = xb_s[:, LANES * g:LANES * (g + 1)]
        acc = win
        k = 1
        while k < w:
            acc = acc + pltpu.roll(acc, k, axis=0)
            k *= 2
        cnt = jnp.minimum(t_abs + 1, w).astype(F32)
        pooled = acc[POOL_HALO:, :] * (1.0 / cnt)
        diff = pooled - win[POOL_HALO:, :]
        yb = _dot(diff.astype(BF16), bw_ref[g]) * bscale_ref[:, LANES * g:LANES * (g + 1)]
        yb = yb * _silu(z_s[:, E_GB + LANES * g:E_GB + LANES * (g + 1)])
        y_s[:, A_WIDTH + LANES * g:A_WIDTH + LANES * (g + 1)] = yb.astype(BF16)

    _memory_attention(z_s, E_QM, E_GM, kt_ref, vm_ref, y_s, A_WIDTH + B_WIDTH)

    o_ref[...] = h_ref[...] + _dot(y_s[...], wout_ref[...])


def _sub_tiles(ref):
    n = ref.shape[0]
    return [(ref.at[k], ref.at[(k - 1) % n]) for k in range(n)]


def _even_kernel(h_ref, g_ref, win_ref, wout_ref, lng_ref, lnb_ref, w2_ref, abias_ref, bw_ref,
                 bscale_ref, kt_ref, vm_ref, o_ref, xn_s, z_s, y_s, sg_s, xb_s):
    t = pl.program_id(1)
    nsub, sub = xn_s.shape[0], xn_s.shape[1]

    @pl.when(t == 0)
    def _():
        xb_s[nsub - 1, sub:sub + POOL_HALO, :] = jnp.zeros((POOL_HALO, B_WIDTH), F32)

    for k, (xb, xb_prev) in enumerate(_sub_tiles(xb_s)):
        rows = pl.ds(k * sub, sub)
        _even_tile(t * (nsub * sub) + k * sub, h_ref.at[rows], g_ref, win_ref, wout_ref, lng_ref, lnb_ref,
                   w2_ref, abias_ref, bw_ref, bscale_ref, kt_ref, vm_ref, o_ref.at[rows],
                   xn_s.at[k], z_s.at[k], y_s.at[k], sg_s.at[k], xb, xb_prev)


def _odd_tile(first, sink_row, h_ref, g_ref, win_ref, wout_ref, qg_ref, kg_ref, sink_ref, dw_ref, dwb_ref, dlg_ref,
              dlb_ref, pw_ref, kt_ref, vm_ref, o_ref, xn_s, z_s, y_s, cv_s, hb_s, kb_s, vb_s,
              hb_prev, kb_prev, vb_prev):
    tile = h_ref.shape[0]
    nb = tile // BLOCK
    hb_s[0, 0:CONV_HALO, :] = hb_prev[0, tile:tile + CONV_HALO, :]
    kb_s[:, 0:BLOCK, :] = kb_prev[:, tile:tile + BLOCK, :]
    vb_s[:, 0:BLOCK, :] = vb_prev[:, tile:tile + BLOCK, :]

    halves = [slice(r, r + tile // 2) for r in (0, tile // 2)]
    for rows in halves:
        h = h_ref[rows, :]
        xn_s[rows, :] = (h * lax.rsqrt(jnp.mean(h * h, axis=-1, keepdims=True) + EPS) * g_ref[...]).astype(BF16)
        z_s[rows, O_DA:O_GD] = _dot(xn_s[rows, :], win_ref[:, O_DA:O_GD])
    for c0, c1 in ((O_GD, O_QM), (O_KC, O_GC), (O_QC, O_KC), (O_GC, O_DA), (O_QM, ODD_IN)):
        for rows in halves:
            z_s[rows, c0:c1] = _dot(xn_s[rows, :], win_ref[:, c0:c1])

    lo = _lo_lanes((tile, LANES))
    k = z_s[:, O_KC:O_KC + C_KV_WIDTH]
    kn = k * _pair_rs(k) * (kg_ref[...] * qg_ref[...] * (HEAD_DIM ** -0.5 * LOG2E))
    v = z_s[:, O_VC:O_VC + C_KV_WIDTH]
    for buf, x in ((kb_s, kn), (vb_s, v)):
        xsw = pltpu.roll(x, HEAD_DIM, axis=1)
        buf[0, BLOCK:BLOCK + tile, :] = jnp.where(lo, x, 0.0).astype(BF16)
        buf[1, BLOCK:BLOCK + tile, :] = jnp.where(lo, 0.0, x).astype(BF16)
        buf[2, BLOCK:BLOCK + tile, :] = jnp.where(lo, xsw, 0.0).astype(BF16)
        buf[3, BLOCK:BLOCK + tile, :] = jnp.where(lo, 0.0, xsw).astype(BF16)
    variant = ((0, 3), (2, 1))

    qi = lax.broadcasted_iota(jnp.int32, (BLOCK, 2 * BLOCK), 0)
    cj = lax.broadcasted_iota(jnp.int32, (BLOCK, 2 * BLOCK), 1)
    band = (cj - qi >= 1) & (cj - qi <= BLOCK)
    first_key = first * BLOCK
    neg = jnp.finfo(F32).min
    for j in range(nb):
        valid = band & (cj >= first_key) if j == 0 else band
        for s in range(C_Q_WIDTH // LANES):
            q = z_s[BLOCK * j:BLOCK * (j + 1), O_QC + LANES * s:O_QC + LANES * (s + 1)]
            qn = (q * _pair_rs(q)).astype(BF16)
            acc = None
            for pos in range(2):
                hd = 2 * s + pos
                var = variant[hd // 4][pos]
                sc = _dot_nt(qn, kb_s[var, BLOCK * j:BLOCK * (j + 2), :])
                sc = jnp.where(valid, sc, neg)
                sink = sink_ref[sink_row, hd] * LOG2E
                m = jnp.maximum(jnp.max(sc, axis=-1, keepdims=True), sink)
                e = jnp.exp2(sc - m)
                l = jnp.sum(e, axis=-1, keepdims=True) + jnp.exp2(sink - m)
                o = _dot(e.astype(BF16), vb_s[var, BLOCK * j:BLOCK * (j + 2), :]) * (1.0 / l)
                acc = o if acc is None else acc + o
            gate = _silu(z_s[BLOCK * j:BLOCK * (j + 1), O_GC + LANES * s:O_GC + LANES * (s + 1)])
            y_s[BLOCK * j:BLOCK * (j + 1), LANES * s:LANES * (s + 1)] = (acc * gate).astype(BF16)

    for rows in halves:
        hb_s[0, CONV_HALO + rows.start:CONV_HALO + rows.stop, :] = (
            z_s[rows, O_DA:O_DA + D_WIDTH] * _sigmoid(z_s[rows, O_DB:O_DB + D_WIDTH]))
    shifted_rows = CONV_HALO + tile - SUBLANES
    for s in range(D_WIDTH // LANES):
        x = hb_s[0, :, LANES * s:LANES * (s + 1)]
        for p in range(1, SUBLANES):
            hb_s[p, 0:shifted_rows, LANES * s:LANES * (s + 1)] = pltpu.roll(
                x, CONV_HALO + tile - p, axis=0)[0:shifted_rows, :]
    base = CONV_HALO - (CONV_WIDTH - 1)
    for s in range(D_WIDTH // LANES):
        for j in range(nb):
            acc = jnp.broadcast_to(dwb_ref[:, LANES * s:LANES * (s + 1)], (BLOCK, LANES))
            for tap in range(CONV_WIDTH):
                p, r = (base + tap) % SUBLANES, (base + tap) // SUBLANES * SUBLANES
                x = hb_s[p, BLOCK * j + r:BLOCK * (j + 1) + r, LANES * s:LANES * (s + 1)]
                acc = acc + x * dw_ref[tap:tap + 1, LANES * s:LANES * (s + 1)]
            cv_s[BLOCK * j:BLOCK * (j + 1), LANES * s:LANES * (s + 1)] = acc
    for rows in halves:
        hd_ = _silu(_layer_norm(cv_s[rows, :], dlg_ref[...], dlb_ref[...]))
        yd = _dot(hd_.astype(BF16), pw_ref[...]) * _silu(z_s[rows, O_GD:O_GD + D_WIDTH])
        y_s[rows, C_Q_WIDTH:C_Q_WIDTH + D_WIDTH] = yd.astype(BF16)

    for rows in halves:
        _memory_attention(z_s, O_QM, O_GM, kt_ref, vm_ref, y_s, C_Q_WIDTH + D_WIDTH, rows)
    for rows in halves:
        o_ref[rows, :] = h_ref[rows, :] + _dot(y_s[rows, :], wout_ref[...])


def _odd_kernel(sink_row, h_ref, g_ref, win_ref, wout_ref, qg_ref, kg_ref, sink_ref, dw_ref, dwb_ref, dlg_ref,
                dlb_ref, pw_ref, kt_ref, vm_ref, o_ref, xn_s, z_s, y_s, cv_s, hb_s, kb_s, vb_s):
    t = pl.program_id(1)
    nsub, sub = xn_s.shape[0], xn_s.shape[1]

    @pl.when(t == 0)
    def _():
        hb_s[nsub - 1, 0, sub:sub + CONV_HALO, :] = jnp.zeros((CONV_HALO, D_WIDTH), F32)
        kb_s[nsub - 1, :, sub:sub + BLOCK, :] = jnp.zeros((4, BLOCK, LANES), BF16)
        vb_s[nsub - 1, :, sub:sub + BLOCK, :] = jnp.zeros((4, BLOCK, LANES), BF16)

    halos = zip(_sub_tiles(hb_s), _sub_tiles(kb_s), _sub_tiles(vb_s))
    for k, ((hb, hb_prev), (kb, kb_prev), (vb, vb_prev)) in enumerate(halos):
        rows = pl.ds(k * sub, sub)
        first = (t == 0).astype(jnp.int32) if k == 0 else 0
        _odd_tile(first, sink_row, h_ref.at[rows], g_ref, win_ref, wout_ref, qg_ref, kg_ref, sink_ref, dw_ref,
                  dwb_ref, dlg_ref, dlb_ref, pw_ref, kt_ref, vm_ref, o_ref.at[rows], xn_s.at[k], z_s.at[k],
                  y_s.at[k], cv_s.at[k], hb, kb, vb, hb_prev, kb_prev, vb_prev)


def _mem_prep_kernel(mem_ref, g_ref, w_ref, qn_ref, kn_ref, kt_ref, vm_ref):
    m = mem_ref[...]
    mn = (m * lax.rsqrt(jnp.mean(m * m, axis=-1, keepdims=True) + EPS) * g_ref[...]).astype(BF16)
    zeros = jnp.zeros((HEAD_DIM, N_MEM), F32)
    lo = _lo_lanes((N_MEM, LANES))
    for layer in range(w_ref.shape[0]):
        kv = _dot(mn, w_ref[layer])
        kt = kv[:, 0:M_WIDTH].T
        gain = kn_ref[layer] * qn_ref[layer] * (HEAD_DIM ** -0.5 * LOG2E)
        for hd in range(M_HEADS):
            kh = kt[HEAD_DIM * hd:HEAD_DIM * (hd + 1), :]
            kh = kh * lax.rsqrt(jnp.mean(kh * kh, axis=0, keepdims=True) + EPS) * gain
            parts = [kh, zeros] if hd % 2 == 0 else [zeros, kh]
            kt_ref[layer, hd] = jnp.concatenate(parts, axis=0).astype(BF16)
            s = hd // 2
            v = kv[:, M_WIDTH + LANES * s:M_WIDTH + LANES * (s + 1)]
            vm_ref[layer, hd] = (jnp.where(lo, v, 0.0) if hd % 2 == 0 else jnp.where(lo, 0.0, v)).astype(BF16)


def _mem_prep(mem, mem_norm_g, w_mem_kv, m_qnorm, m_knorm):
    depth, batch = w_mem_kv.shape[0], mem.shape[0]
    whole = lambda shape: pl.BlockSpec(shape, lambda b: (0,) * len(shape))
    return pl.pallas_call(
        _mem_prep_kernel,
        grid=(batch,),
        in_specs=[
            pl.BlockSpec((None, N_MEM, D_MODEL), lambda b: (b, 0, 0)),
            whole((1, D_MODEL)),
            whole((depth, D_MODEL, 2 * M_WIDTH)),
            whole((depth, HEAD_DIM, 1)),
            whole((depth, HEAD_DIM, 1)),
        ],
        out_specs=[
            pl.BlockSpec((depth, None, M_HEADS, LANES, N_MEM), lambda b: (0, b, 0, 0, 0)),
            pl.BlockSpec((depth, None, M_HEADS, N_MEM, LANES), lambda b: (0, b, 0, 0, 0)),
        ],
        out_shape=[
            jax.ShapeDtypeStruct((depth, batch, M_HEADS, LANES, N_MEM), BF16),
            jax.ShapeDtypeStruct((depth, batch, M_HEADS, N_MEM, LANES), BF16),
        ],
        compiler_params=pltpu.CompilerParams(dimension_semantics=("arbitrary",)),
        name="mem_prep",
    )(mem, mem_norm_g.reshape(1, D_MODEL), w_mem_kv.astype(BF16),
      m_qnorm.reshape(depth, HEAD_DIM, 1), m_knorm.reshape(depth, HEAD_DIM, 1))


def _layer_of(stacked, index):
    tail = stacked.shape[1:]
    return pl.BlockSpec((None,) + tail, lambda b, t: (index,) + (0,) * len(tail), pipeline_mode=pl.Buffered(1))


def _rows(x):
    return x.reshape(x.shape[0], 1, x.shape[1])


def _layer_call(body, name, h, in_width, tile, nsub, scratch, operands, layer, kt, vm):
    batch, seq, _ = h.shape
    sub = tile // nsub
    h_spec = pl.BlockSpec((None, tile, D_MODEL), lambda b, t: (b, t, 0))
    mem_specs = [
        pl.BlockSpec((None, None, M_HEADS, LANES, N_MEM), lambda b, t: (layer, b, 0, 0, 0)),
        pl.BlockSpec((None, None, M_HEADS, N_MEM, LANES), lambda b, t: (layer, b, 0, 0, 0)),
    ]
    return pl.pallas_call(
        body,
        grid=(batch, seq // tile),
        in_specs=[h_spec] + [spec for _, spec in operands] + mem_specs,
        out_specs=h_spec,
        out_shape=jax.ShapeDtypeStruct(h.shape, h.dtype),
        scratch_shapes=[
            pltpu.VMEM((nsub, sub, D_MODEL), BF16),
            pltpu.VMEM((nsub, sub, in_width), F32),
            pltpu.VMEM((nsub, sub, OUT_WIDTH), BF16),
        ] + scratch,
        compiler_params=pltpu.CompilerParams(
            dimension_semantics=("arbitrary", "arbitrary"), vmem_limit_bytes=VMEM_LIMIT),
        name=name,
    )(h, *[x for x, _ in operands], kt, vm)


def kernel(x, mem, norm_g, mem_norm_g, w_mem_kv, m_qnorm, m_knorm, w_in_even, w_out_even, a_ln_g, a_ln_b, a_ws, a_bs, b_w, b_scale, w_in_odd, w_out_odd, c_qnorm, c_knorm, c_sink, d_dw, d_dw_b, d_ln_g, d_ln_b, d_pw):
    kt, vm = _mem_prep(mem, mem_norm_g, w_mem_kv, m_qnorm, m_knorm)

    n_even, heads = a_ws.shape[0], a_ws.shape[1]
    norm_rows = _rows(norm_g)
    even = dict(
        w_in=w_in_even.astype(BF16), w_out=w_out_even.astype(BF16), ln_g=_rows(a_ln_g), ln_b=_rows(a_ln_b),
        w2=a_ws.reshape(n_even, heads // 2, 2, CHUNK, CHUNK).transpose(0, 1, 3, 2, 4)
               .reshape(n_even, heads // 2, CHUNK, 2 * CHUNK).astype(BF16),
        abias=jnp.repeat(a_bs.reshape(n_even, heads // 2, 2, CHUNK).transpose(0, 1, 3, 2), HEAD_DIM, axis=3),
        bw=b_w.astype(BF16), bscale=_rows(b_scale))
    odd = dict(
        w_in=w_in_odd.astype(BF16), w_out=w_out_odd.astype(BF16),
        qg=_rows(jnp.tile(c_qnorm, (1, 2))), kg=_rows(jnp.tile(c_knorm, (1, 2))),
        dw=d_dw, dwb=_rows(d_dw_b), ln_g=_rows(d_ln_g), ln_b=_rows(d_ln_b), pw=d_pw.astype(BF16))
    even_scratch = [pltpu.VMEM((EVEN_SUB_TILES, EVEN_TILE // EVEN_SUB_TILES, A_WIDTH), F32),
                    pltpu.VMEM((EVEN_SUB_TILES, POOL_HALO + EVEN_TILE // EVEN_SUB_TILES, B_WIDTH), F32)]
    odd_sub = TILE // ODD_SUB_TILES
    odd_scratch = [pltpu.VMEM((ODD_SUB_TILES, odd_sub, D_WIDTH), F32),
                   pltpu.VMEM((ODD_SUB_TILES, SUBLANES, CONV_HALO + odd_sub, D_WIDTH), F32),
                   pltpu.VMEM((ODD_SUB_TILES, 4, BLOCK + odd_sub, LANES), BF16),
                   pltpu.VMEM((ODD_SUB_TILES, 4, BLOCK + odd_sub, LANES), BF16)]

    h = x
    for layer in range(w_mem_kv.shape[0]):
        i = layer // 2
        norm = (norm_rows, _layer_of(norm_rows, layer))
        if layer % 2 == 0:
            names = ("w_in", "w_out", "ln_g", "ln_b", "w2", "abias", "bw", "bscale")
            operands = [norm] + [(even[n], _layer_of(even[n], i)) for n in names]
            h = _layer_call(_even_kernel, "even_layer", h, EVEN_IN, EVEN_TILE, EVEN_SUB_TILES, even_scratch, operands,
                            layer, kt, vm)
        else:
            operands = [norm] + [(odd[n], _layer_of(odd[n], i)) for n in ("w_in", "w_out", "qg", "kg")]
            operands.append((c_sink, pl.BlockSpec(memory_space=pltpu.SMEM)))
            operands += [(odd[n], _layer_of(odd[n], i)) for n in ("dw", "dwb", "ln_g", "ln_b", "pw")]
            h = _layer_call(functools.partial(_odd_kernel, i), "odd_layer", h, ODD_IN, TILE, ODD_SUB_TILES,
                            odd_scratch, operands, layer, kt, vm)
    return h
```

```python
import functools

import jax
import jax.numpy as jnp
from jax import lax
from jax.experimental import pallas as pl
from jax.experimental.pallas import tpu as pltpu

D_MODEL = 1024
N_MEM = 256
HEAD_DIM = 64
EPS = 1e-6
LOG2E = 1.4426950408889634
A_WIDTH = 512
CHUNK = 128
POOL_WINDOWS = (2, 4, 8, 16)
B_WIDTH = 512
C_Q_WIDTH = 512
C_KV_WIDTH = 128
BLOCK = 128
D_WIDTH = 512
CONV_WIDTH = 31
M_HEADS = 4
M_WIDTH = 256
OUT_WIDTH = 1280
EVEN_IN = 3072
ODD_IN = 3328

LANES = 128
SUBLANES = 8
POOL_HALO = 16
CONV_HALO = 32
TILE = 512
EVEN_TILE = 1024
EVEN_SUB_TILES = 4
ODD_SUB_TILES = 1
VMEM_LIMIT = 56 * 1024 * 1024

F32 = jnp.float32
BF16 = jnp.bfloat16

E_U, E_V, E_GA, E_XB, E_GB, E_QM, E_GM = 0, 512, 1024, 1536, 2048, 2560, 2816
O_QC, O_KC, O_VC, O_GC, O_DA, O_DB, O_GD, O_QM, O_GM = 0, 512, 640, 768, 1280, 1792, 2304, 2816, 3072


def _dot(a, b):
    return jnp.dot(a, b, preferred_element_type=F32)


def _dot_nt(a, b):
    return lax.dot_general(a, b, (((1,), (1,)), ((), ())), preferred_element_type=F32)


def _sigmoid(x):
    return 0.5 * jnp.tanh(0.5 * x) + 0.5


def _silu(x):
    hx = 0.5 * x
    return hx * jnp.tanh(hx) + hx


def _lo_lanes(shape):
    return lax.broadcasted_iota(jnp.int32, shape, len(shape) - 1) < HEAD_DIM


def _pair_rs(x):
    lo = _lo_lanes(x.shape)
    x2 = x * x
    s_lo = jnp.sum(jnp.where(lo, x2, 0.0), axis=-1, keepdims=True)
    s_hi = jnp.sum(jnp.where(lo, 0.0, x2), axis=-1, keepdims=True)
    return lax.rsqrt(jnp.where(lo, s_lo, s_hi) * (1.0 / HEAD_DIM) + EPS)


def _layer_norm(x, g, b):
    mu = jnp.mean(x, axis=-1, keepdims=True)
    xc = x - mu
    return xc * lax.rsqrt(jnp.mean(xc * xc, axis=-1, keepdims=True) + EPS) * g + b


def _project_in(h_ref, g_ref, win_ref, xn_s, z_s, groups):
    h = h_ref[...]
    xn = h * lax.rsqrt(jnp.mean(h * h, axis=-1, keepdims=True) + EPS) * g_ref[...]
    xn_s[...] = xn.astype(BF16)
    for lo, hi in groups:
        z_s[:, lo:hi] = _dot(xn_s[...], win_ref[:, lo:hi])


def _memory_attention(z_s, q_off, g_off, kt_ref, vm_ref, y_s, y_off, rows=slice(None)):
    for s in range(M_WIDTH // LANES):
        q = z_s[rows, q_off + LANES * s:q_off + LANES * (s + 1)]
        qn = (q * _pair_rs(q)).astype(BF16)
        acc = None
        for pos in range(2):
            hd = 2 * s + pos
            sc = _dot(qn, kt_ref[hd])
            m = jnp.max(sc, axis=-1, keepdims=True)
            e = jnp.exp2(sc - m)
            l = jnp.sum(e, axis=-1, keepdims=True)
            o = _dot(e.astype(BF16), vm_ref[hd]) * (1.0 / l)
            acc = o if acc is None else acc + o
        gate = _silu(z_s[rows, g_off + LANES * s:g_off + LANES * (s + 1)])
        y_s[rows, y_off + LANES * s:y_off + LANES * (s + 1)] = (acc * gate).astype(BF16)


def _even_tile(t_abs0, h_ref, g_ref, win_ref, wout_ref, lng_ref, lnb_ref, w2_ref, abias_ref, bw_ref,
               bscale_ref, kt_ref, vm_ref, o_ref, xn_s, z_s, y_s, sg_s, xb_s, xb_prev):
    tile = h_ref.shape[0]
    nc = tile // CHUNK
    xb_s[0:POOL_HALO, :] = xb_prev[tile:tile + POOL_HALO, :]

    _project_in(h_ref, g_ref, win_ref, xn_s, z_s, ((E_XB, E_QM), (E_U, E_XB), (E_QM, EVEN_IN)))

    vn = _layer_norm(z_s[:, E_V:E_V + A_WIDTH], lng_ref[...], lnb_ref[...])
    lo = _lo_lanes((CHUNK, LANES))
    row = lax.broadcasted_iota(jnp.int32, (CHUNK, 2 * CHUNK), 0)
    col = lax.broadcasted_iota(jnp.int32, (CHUNK, 2 * CHUNK), 1)
    tril2 = (col & (CHUNK - 1)) <= row
    for s in range(A_WIDTH // LANES):
        tops, bots = [], []
        for c in range(nc):
            blk = vn[CHUNK * c:CHUNK * (c + 1), LANES * s:LANES * (s + 1)]
            tops.append(jnp.where(lo, blk, 0.0).astype(BF16))
            bots.append(jnp.where(lo, 0.0, blk).astype(BF16))
        rhs = jnp.concatenate([jnp.concatenate(tops, axis=1), jnp.concatenate(bots, axis=1)], axis=0)
        w2 = jnp.where(tril2, w2_ref[s], jnp.zeros((), BF16))
        sg = _dot(w2, rhs)
        for c in range(nc):
            sg_s[CHUNK * c:CHUNK * (c + 1), LANES * s:LANES * (s + 1)] = (
                sg[:, LANES * c:LANES * (c + 1)] + abias_ref[s])
    ya = z_s[:, E_U:E_U + A_WIDTH] * sg_s[...] * _silu(z_s[:, E_GA:E_GA + A_WIDTH])
    y_s[:, 0:A_WIDTH] = ya.astype(BF16)

    xb_s[POOL_HALO:POOL_HALO + tile, :] = z_s[:, E_XB:E_XB + B_WIDTH]
    t_abs = t_abs0 + lax.broadcasted_iota(jnp.int32, (tile, 1), 0)
    for g, w in enumerate(POOL_WINDOWS):
        win = xb_s[:, LANES * g:LANES * (g + 1)]
        acc = win
        k = 1
        while k < w:
            acc = acc + pltpu.roll(acc, k, axis=0)
            k *= 2
        cnt = jnp.minimum(t_abs + 1, w).astype(F32)
        pooled = acc[POOL_HALO:, :] * (1.0 / cnt)
        diff = pooled - win[POOL_HALO:, :]
        yb = _dot(diff.astype(BF16), bw_ref[g]) * bscale_ref[:, LANES * g:LANES * (g + 1)]
        yb = yb * _silu(z_s[:, E_GB + LANES * g:E_GB + LANES * (g + 1)])
        y_s[:, A_WIDTH + LANES * g:A_WIDTH + LANES * (g + 1)] = yb.astype(BF16)

    _memory_attention(z_s, E_QM, E_GM, kt_ref, vm_ref, y_s, A_WIDTH + B_WIDTH)

    o_ref[...] = h_ref[...] + _dot(y_s[...], wout_ref[...])


def _sub_tiles(ref):
    n = ref.shape[0]
    return [(ref.at[k], ref.at[(k - 1) % n]) for k in range(n)]


def _even_kernel(h_ref, g_ref, win_ref, wout_ref, lng_ref, lnb_ref, w2_ref, abias_ref, bw_ref,
                 bscale_ref, kt_ref, vm_ref, o_ref, xn_s, z_s, y_s, sg_s, xb_s):
    t = pl.program_id(1)
    nsub, sub = xn_s.shape[0], xn_s.shape[1]

    @pl.when(t == 0)
    def _():
        xb_s[nsub - 1, sub:sub + POOL_HALO, :] = jnp.zeros((POOL_HALO, B_WIDTH), F32)

    for k, (xb, xb_prev) in enumerate(_sub_tiles(xb_s)):
        rows = pl.ds(k * sub, sub)
        _even_tile(t * (nsub * sub) + k * sub, h_ref.at[rows], g_ref, win_ref, wout_ref, lng_ref, lnb_ref,
                   w2_ref, abias_ref, bw_ref, bscale_ref, kt_ref, vm_ref, o_ref.at[rows],
                   xn_s.at[k], z_s.at[k], y_s.at[k], sg_s.at[k], xb, xb_prev)


def _odd_tile(first, sink_row, h_ref, g_ref, win_ref, wout_ref, qg_ref, kg_ref, sink_ref, dw_ref, dwb_ref, dlg_ref,
              dlb_ref, pw_ref, kt_ref, vm_ref, o_ref, xn_s, z_s, y_s, cv_s, hb_s, kb_s, vb_s,
              hb_prev, kb_prev, vb_prev):
    tile = h_ref.shape[0]
    nb = tile // BLOCK
    hb_s[0, 0:CONV_HALO, :] = hb_prev[0, tile:tile + CONV_HALO, :]
    kb_s[:, 0:BLOCK, :] = kb_prev[:, tile:tile + BLOCK, :]
    vb_s[:, 0:BLOCK, :] = vb_prev[:, tile:tile + BLOCK, :]

    halves = [slice(r, r + tile // 2) for r in (0, tile // 2)]
    for rows in halves:
        h = h_ref[rows, :]
        xn_s[rows, :] = (h * lax.rsqrt(jnp.mean(h * h, axis=-1, keepdims=True) + EPS) * g_ref[...]).astype(BF16)
        z_s[rows, O_DA:O_GD] = _dot(xn_s[rows, :], win_ref[:, O_DA:O_GD])
    for c0, c1 in ((O_GD, O_QM), (O_KC, O_GC), (O_QC, O_KC), (O_GC, O_DA), (O_QM, ODD_IN)):
        for rows in halves:
            z_s[rows, c0:c1] = _dot(xn_s[rows, :], win_ref[:, c0:c1])

    lo = _lo_lanes((tile, LANES))
    k = z_s[:, O_KC:O_KC + C_KV_WIDTH]
    kn = k * _pair_rs(k) * (kg_ref[...] * qg_ref[...] * (HEAD_DIM ** -0.5 * LOG2E))
    v = z_s[:, O_VC:O_VC + C_KV_WIDTH]
    for buf, x in ((kb_s, kn), (vb_s, v)):
        xsw = pltpu.roll(x, HEAD_DIM, axis=1)
        buf[0, BLOCK:BLOCK + tile, :] = jnp.where(lo, x, 0.0).astype(BF16)
        buf[1, BLOCK:BLOCK + tile, :] = jnp.where(lo, 0.0, x).astype(BF16)
        buf[2, BLOCK:BLOCK + tile, :] = jnp.where(lo, xsw, 0.0).astype(BF16)
        buf[3, BLOCK:BLOCK + tile, :] = jnp.where(lo, 0.0, xsw).astype(BF16)
    variant = ((0, 3), (2, 1))

    qi = lax.broadcasted_iota(jnp.int32, (BLOCK, 2 * BLOCK), 0)
    cj = lax.broadcasted_iota(jnp.int32, (BLOCK, 2 * BLOCK), 1)
    band = (cj - qi >= 1) & (cj - qi <= BLOCK)
    first_key = first * BLOCK
    neg = jnp.finfo(F32).min
    for j in range(nb):
        valid = band & (cj >= first_key) if j == 0 else band
        for s in range(C_Q_WIDTH // LANES):
            q = z_s[BLOCK * j:BLOCK * (j + 1), O_QC + LANES * s:O_QC + LANES * (s + 1)]
            qn = (q * _pair_rs(q)).astype(BF16)
            acc = None
            for pos in range(2):
                hd = 2 * s + pos
                var = variant[hd // 4][pos]
                sc = _dot_nt(qn, kb_s[var, BLOCK * j:BLOCK * (j + 2), :])
                sc = jnp.where(valid, sc, neg)
                sink = sink_ref[sink_row, hd] * LOG2E
                m = jnp.maximum(jnp.max(sc, axis=-1, keepdims=True), sink)
                e = jnp.exp2(sc - m)
                l = jnp.sum(e, axis=-1, keepdims=True) + jnp.exp2(sink - m)
                o = _dot(e.astype(BF16), vb_s[var, BLOCK * j:BLOCK * (j + 2), :]) * (1.0 / l)
                acc = o if acc is None else acc + o
            gate = _silu(z_s[BLOCK * j:BLOCK * (j + 1), O_GC + LANES * s:O_GC + LANES * (s + 1)])
            y_s[BLOCK * j:BLOCK * (j + 1), LANES * s:LANES * (s + 1)] = (acc * gate).astype(BF16)

    for rows in halves:
        _memory_attention(z_s, O_QM, O_GM, kt_ref, vm_ref, y_s, C_Q_WIDTH + D_WIDTH, rows)

    for rows in halves:
        hb_s[0, CONV_HALO + rows.start:CONV_HALO + rows.stop, :] = (
            z_s[rows, O_DA:O_DA + D_WIDTH] * _sigmoid(z_s[rows, O_DB:O_DB + D_WIDTH]))
    shifted_rows = CONV_HALO + tile - SUBLANES
    for s in range(D_WIDTH // LANES):
        x = hb_s[0, :, LANES * s:LANES * (s + 1)]
        for p in range(1, SUBLANES):
            hb_s[p, 0:shifted_rows, LANES * s:LANES * (s + 1)] = pltpu.roll(
                x, CONV_HALO + tile - p, axis=0)[0:shifted_rows, :]
    base = CONV_HALO - (CONV_WIDTH - 1)
    for s in range(D_WIDTH // LANES):
        for j in range(nb):
            acc = jnp.broadcast_to(dwb_ref[:, LANES * s:LANES * (s + 1)], (BLOCK, LANES))
            for tap in range(CONV_WIDTH):
                p, r = (base + tap) % SUBLANES, (base + tap) // SUBLANES * SUBLANES
                x = hb_s[p, BLOCK * j + r:BLOCK * (j + 1) + r, LANES * s:LANES * (s + 1)]
                acc = acc + x * dw_ref[tap:tap + 1, LANES * s:LANES * (s + 1)]
            cv_s[BLOCK * j:BLOCK * (j + 1), LANES * s:LANES * (s + 1)] = acc
    for rows in halves:
        hd_ = _silu(_layer_norm(cv_s[rows, :], dlg_ref[...], dlb_ref[...]))
        yd = _dot(hd_.astype(BF16), pw_ref[...]) * _silu(z_s[rows, O_GD:O_GD + D_WIDTH])
        y_s[rows, C_Q_WIDTH:C_Q_WIDTH + D_WIDTH] = yd.astype(BF16)

    for rows in halves:
        o_ref[rows, :] = h_ref[rows, :] + _dot(y_s[rows, :], wout_ref[...])


def _odd_kernel(sink_row, h_ref, g_ref, win_ref, wout_ref, qg_ref, kg_ref, sink_ref, dw_ref, dwb_ref, dlg_ref,
                dlb_ref, pw_ref, kt_ref, vm_ref, o_ref, xn_s, z_s, y_s, cv_s, hb_s, kb_s, vb_s):
    t = pl.program_id(1)
    nsub, sub = xn_s.shape[0], xn_s.shape[1]

    @pl.when(t == 0)
    def _():
        hb_s[nsub - 1, 0, sub:sub + CONV_HALO, :] = jnp.zeros((CONV_HALO, D_WIDTH), F32)
        kb_s[nsub - 1, :, sub:sub + BLOCK, :] = jnp.zeros((4, BLOCK, LANES), BF16)
        vb_s[nsub - 1, :, sub:sub + BLOCK, :] = jnp.zeros((4, BLOCK, LANES), BF16)

    halos = zip(_sub_tiles(hb_s), _sub_tiles(kb_s), _sub_tiles(vb_s))
    for k, ((hb, hb_prev), (kb, kb_prev), (vb, vb_prev)) in enumerate(halos):
        rows = pl.ds(k * sub, sub)
        first = (t == 0).astype(jnp.int32) if k == 0 else 0
        _odd_tile(first, sink_row, h_ref.at[rows], g_ref, win_ref, wout_ref, qg_ref, kg_ref, sink_ref, dw_ref,
                  dwb_ref, dlg_ref, dlb_ref, pw_ref, kt_ref, vm_ref, o_ref.at[rows], xn_s.at[k], z_s.at[k],
                  y_s.at[k], cv_s.at[k], hb, kb, vb, hb_prev, kb_prev, vb_prev)


def _mem_prep_kernel(mem_ref, g_ref, w_ref, qn_ref, kn_ref, kt_ref, vm_ref):
    m = mem_ref[...]
    mn = (m * lax.rsqrt(jnp.mean(m * m, axis=-1, keepdims=True) + EPS) * g_ref[...]).astype(BF16)
    zeros = jnp.zeros((HEAD_DIM, N_MEM), F32)
    lo = _lo_lanes((N_MEM, LANES))
    for layer in range(w_ref.shape[0]):
        kv = _dot(mn, w_ref[layer])
        kt = kv[:, 0:M_WIDTH].T
        gain = kn_ref[layer] * qn_ref[layer] * (HEAD_DIM ** -0.5 * LOG2E)
        for hd in range(M_HEADS):
            kh = kt[HEAD_DIM * hd:HEAD_DIM * (hd + 1), :]
            kh = kh * lax.rsqrt(jnp.mean(kh * kh, axis=0, keepdims=True) + EPS) * gain
            parts = [kh, zeros] if hd % 2 == 0 else [zeros, kh]
            kt_ref[layer, hd] = jnp.concatenate(parts, axis=0).astype(BF16)
            s = hd // 2
            v = kv[:, M_WIDTH + LANES * s:M_WIDTH + LANES * (s + 1)]
            vm_ref[layer, hd] = (jnp.where(lo, v, 0.0) if hd % 2 == 0 else jnp.where(lo, 0.0, v)).astype(BF16)


def _mem_prep(mem, mem_norm_g, w_mem_kv, m_qnorm, m_knorm):
    depth, batch = w_mem_kv.shape[0], mem.shape[0]
    whole = lambda shape: pl.BlockSpec(shape, lambda b: (0,) * len(shape))
    return pl.pallas_call(
        _mem_prep_kernel,
        grid=(batch,),
        in_specs=[
            pl.BlockSpec((None, N_MEM, D_MODEL), lambda b: (b, 0, 0)),
            whole((1, D_MODEL)),
            whole((depth, D_MODEL, 2 * M_WIDTH)),
            whole((depth, HEAD_DIM, 1)),
            whole((depth, HEAD_DIM, 1)),
        ],
        out_specs=[
            pl.BlockSpec((depth, None, M_HEADS, LANES, N_MEM), lambda b: (0, b, 0, 0, 0)),
            pl.BlockSpec((depth, None, M_HEADS, N_MEM, LANES), lambda b: (0, b, 0, 0, 0)),
        ],
        out_shape=[
            jax.ShapeDtypeStruct((depth, batch, M_HEADS, LANES, N_MEM), BF16),
            jax.ShapeDtypeStruct((depth, batch, M_HEADS, N_MEM, LANES), BF16),
        ],
        compiler_params=pltpu.CompilerParams(dimension_semantics=("arbitrary",)),
        name="mem_prep",
    )(mem, mem_norm_g.reshape(1, D_MODEL), w_mem_kv.astype(BF16),
      m_qnorm.reshape(depth, HEAD_DIM, 1), m_knorm.reshape(depth, HEAD_DIM, 1))


def _layer_of(stacked, index):
    tail = stacked.shape[1:]
    return pl.BlockSpec((None,) + tail, lambda b, t: (index,) + (0,) * len(tail), pipeline_mode=pl.Buffered(1))


def _rows(x):
    return x.reshape(x.shape[0], 1, x.shape[1])


def _layer_call(body, name, h, in_width, tile, nsub, scratch, operands, layer, kt, vm):
    batch, seq, _ = h.shape
    sub = tile // nsub
    h_spec = pl.BlockSpec((None, tile, D_MODEL), lambda b, t: (b, t, 0))
    mem_specs = [
        pl.BlockSpec((None, None, M_HEADS, LANES, N_MEM), lambda b, t: (layer, b, 0, 0, 0)),
        pl.BlockSpec((None, None, M_HEADS, N_MEM, LANES), lambda b, t: (layer, b, 0, 0, 0)),
    ]
    return pl.pallas_call(
        body,
        grid=(batch, seq // tile),
        in_specs=[h_spec] + [spec for _, spec in operands] + mem_specs,
        out_specs=h_spec,
        out_shape=jax.ShapeDtypeStruct(h.shape, h.dtype),
        scratch_shapes=[
            pltpu.VMEM((nsub, sub, D_MODEL), BF16),
            pltpu.VMEM((nsub, sub, in_width), F32),
            pltpu.VMEM((nsub, sub, OUT_WIDTH), BF16),
        ] + scratch,
        compiler_params=pltpu.CompilerParams(
            dimension_semantics=("arbitrary", "arbitrary"), vmem_limit_bytes=VMEM_LIMIT),
        name=name,
    )(h, *[x for x, _ in operands], kt, vm)


def kernel(x, mem, norm_g, mem_norm_g, w_mem_kv, m_qnorm, m_knorm, w_in_even, w_out_even, a_ln_g, a_ln_b, a_ws, a_bs, b_w, b_scale, w_in_odd, w_out_odd, c_qnorm, c_knorm, c_sink, d_dw, d_dw_b, d_ln_g, d_ln_b, d_pw):
    kt, vm = _mem_prep(mem, mem_norm_g, w_mem_kv, m_qnorm, m_knorm)

    n_even, heads = a_ws.shape[0], a_ws.shape[1]
    norm_rows = _rows(norm_g)
    even = dict(
        w_in=w_in_even.astype(BF16), w_out=w_out_even.astype(BF16), ln_g=_rows(a_ln_g), ln_b=_rows(a_ln_b),
        w2=a_ws.reshape(n_even, heads // 2, 2, CHUNK, CHUNK).transpose(0, 1, 3, 2, 4)
               .reshape(n_even, heads // 2, CHUNK, 2 * CHUNK).astype(BF16),
        abias=jnp.repeat(a_bs.reshape(n_even, heads // 2, 2, CHUNK).transpose(0, 1, 3, 2), HEAD_DIM, axis=3),
        bw=b_w.astype(BF16), bscale=_rows(b_scale))
    odd = dict(
        w_in=w_in_odd.astype(BF16), w_out=w_out_odd.astype(BF16),
        qg=_rows(jnp.tile(c_qnorm, (1, 2))), kg=_rows(jnp.tile(c_knorm, (1, 2))),
        dw=d_dw, dwb=_rows(d_dw_b), ln_g=_rows(d_ln_g), ln_b=_rows(d_ln_b), pw=d_pw.astype(BF16))
    even_scratch = [pltpu.VMEM((EVEN_SUB_TILES, EVEN_TILE // EVEN_SUB_TILES, A_WIDTH), F32),
                    pltpu.VMEM((EVEN_SUB_TILES, POOL_HALO + EVEN_TILE // EVEN_SUB_TILES, B_WIDTH), F32)]
    odd_sub = TILE // ODD_SUB_TILES
    odd_scratch = [pltpu.VMEM((ODD_SUB_TILES, odd_sub, D_WIDTH), F32),
                   pltpu.VMEM((ODD_SUB_TILES, SUBLANES, CONV_HALO + odd_sub, D_WIDTH), F32),
                   pltpu.VMEM((ODD_SUB_TILES, 4, BLOCK + odd_sub, LANES), BF16),
                   pltpu.VMEM((ODD_SUB_TILES, 4, BLOCK + odd_sub, LANES), BF16)]

    h = x
    for layer in range(w_mem_kv.shape[0]):
        i = layer // 2
        norm = (norm_rows, _layer_of(norm_rows, layer))
        if layer % 2 == 0:
            names = ("w_in", "w_out", "ln_g", "ln_b", "w2", "abias", "bw", "bscale")
            operands = [norm] + [(even[n], _layer_of(even[n], i)) for n in names]
            h = _layer_call(_even_kernel, "even_layer", h, EVEN_IN, EVEN_TILE, EVEN_SUB_TILES, even_scratch, operands,
                            layer, kt, vm)
        else:
            operands = [norm] + [(odd[n], _layer_of(odd[n], i)) for n in ("w_in", "w_out", "qg", "kg")]
            operands.append((c_sink, pl.BlockSpec(memory_space=pltpu.SMEM)))
            operands += [(odd[n], _layer_of(odd[n], i)) for n in ("dw", "dwb", "ln_g", "ln_b", "pw")]
            h = _layer_call(functools.partial(_odd_kernel, i), "odd_layer", h, ODD_IN, TILE, ODD_SUB_TILES,
                            odd_scratch, operands, layer, kt, vm)
    return h
```

```python
import functools

import jax
import jax.numpy as jnp
from jax import lax
from jax.experimental import pallas as pl
from jax.experimental.pallas import tpu as pltpu

D_MODEL = 1024
N_MEM = 256
HEAD_DIM = 64
EPS = 1e-6
LOG2E = 1.4426950408889634
A_WIDTH = 512
CHUNK = 128
POOL_WINDOWS = (2, 4, 8, 16)
B_WIDTH = 512
C_Q_WIDTH = 512
C_KV_WIDTH = 128
BLOCK = 128
D_WIDTH = 512
CONV_WIDTH = 31
M_HEADS = 4
M_WIDTH = 256
OUT_WIDTH = 1280
EVEN_IN = 3072
ODD_IN = 3328

LANES = 128
SUBLANES = 8
POOL_HALO = 16
CONV_HALO = 32
TILE = 512
EVEN_TILE = 1024
EVEN_SUB_TILES = 2
ODD_SUB_TILES = 1
VMEM_LIMIT = 56 * 1024 * 1024

F32 = jnp.float32
BF16 = jnp.bfloat16

E_U, E_V, E_GA, E_XB, E_GB, E_QM, E_GM = 0, 512, 1024, 1536, 2048, 2560, 2816
O_QC, O_KC, O_VC, O_GC, O_DA, O_DB, O_GD, O_QM, O_GM = 0, 512, 640, 768, 1280, 1792, 2304, 2816, 3072


def _dot(a, b):
    return jnp.dot(a, b, preferred_element_type=F32)


def _dot_nt(a, b):
    return lax.dot_general(a, b, (((1,), (1,)), ((), ())), preferred_element_type=F32)


def _sigmoid(x):
    return 0.5 * jnp.tanh(0.5 * x) + 0.5


def _silu(x):
    hx = 0.5 * x
    return hx * jnp.tanh(hx) + hx


def _lo_lanes(shape):
    return lax.broadcasted_iota(jnp.int32, shape, len(shape) - 1) < HEAD_DIM


def _pair_rs(x):
    lo = _lo_lanes(x.shape)
    x2 = x * x
    s_lo = jnp.sum(jnp.where(lo, x2, 0.0), axis=-1, keepdims=True)
    s_hi = jnp.sum(jnp.where(lo, 0.0, x2), axis=-1, keepdims=True)
    return lax.rsqrt(jnp.where(lo, s_lo, s_hi) * (1.0 / HEAD_DIM) + EPS)


def _layer_norm(x, g, b):
    mu = jnp.mean(x, axis=-1, keepdims=True)
    xc = x - mu
    return xc * lax.rsqrt(jnp.mean(xc * xc, axis=-1, keepdims=True) + EPS) * g + b


def _project_in(h_ref, g_ref, win_ref, xn_s, z_s, groups):
    h = h_ref[...]
    xn = h * lax.rsqrt(jnp.mean(h * h, axis=-1, keepdims=True) + EPS) * g_ref[...]
    xn_s[...] = xn.astype(BF16)
    for lo, hi in groups:
        z_s[:, lo:hi] = _dot(xn_s[...], win_ref[:, lo:hi])


def _memory_attention(z_s, q_off, g_off, kt_ref, vm_ref, y_s, y_off, rows=slice(None)):
    for s in range(M_WIDTH // LANES):
        q = z_s[rows, q_off + LANES * s:q_off + LANES * (s + 1)]
        qn = (q * _pair_rs(q)).astype(BF16)
        acc = None
        for pos in range(2):
            hd = 2 * s + pos
            sc = _dot(qn, kt_ref[hd])
            m = jnp.max(sc, axis=-1, keepdims=True)
            e = jnp.exp2(sc - m)
            l = jnp.sum(e, axis=-1, keepdims=True)
            o = _dot(e.astype(BF16), vm_ref[hd]) * (1.0 / l)
            acc = o if acc is None else acc + o
        gate = _silu(z_s[rows, g_off + LANES * s:g_off + LANES * (s + 1)])
        y_s[rows, y_off + LANES * s:y_off + LANES * (s + 1)] = (acc * gate).astype(BF16)


def _even_tile(t_abs0, h_ref, g_ref, win_ref, wout_ref, lng_ref, lnb_ref, w2_ref, abias_ref, bw_ref,
               bscale_ref, kt_ref, vm_ref, o_ref, xn_s, z_s, y_s, sg_s, xb_s, xb_prev):
    tile = h_ref.shape[0]
    nc = tile // CHUNK
    xb_s[0:POOL_HALO, :] = xb_prev[tile:tile + POOL_HALO, :]

    _project_in(h_ref, g_ref, win_ref, xn_s, z_s, ((E_XB, E_QM), (E_U, E_XB), (E_QM, EVEN_IN)))

    vn = _layer_norm(z_s[:, E_V:E_V + A_WIDTH], lng_ref[...], lnb_ref[...])
    lo = _lo_lanes((CHUNK, LANES))
    row = lax.broadcasted_iota(jnp.int32, (CHUNK, 2 * CHUNK), 0)
    col = lax.broadcasted_iota(jnp.int32, (CHUNK, 2 * CHUNK), 1)
    tril2 = (col & (CHUNK - 1)) <= row
    for s in range(A_WIDTH // LANES):
        tops, bots = [], []
        for c in range(nc):
            blk = vn[CHUNK * c:CHUNK * (c + 1), LANES * s:LANES * (s + 1)]
            tops.append(jnp.where(lo, blk, 0.0).astype(BF16))
            bots.append(jnp.where(lo, 0.0, blk).astype(BF16))
        rhs = jnp.concatenate([jnp.concatenate(tops, axis=1), jnp.concatenate(bots, axis=1)], axis=0)
        w2 = jnp.where(tril2, w2_ref[s], jnp.zeros((), BF16))
        sg = _dot(w2, rhs)
        for c in range(nc):
            sg_s[CHUNK * c:CHUNK * (c + 1), LANES * s:LANES * (s + 1)] = (
                sg[:, LANES * c:LANES * (c + 1)] + abias_ref[s])
    ya = z_s[:, E_U:E_U + A_WIDTH] * sg_s[...] * _silu(z_s[:, E_GA:E_GA + A_WIDTH])
    y_s[:, 0:A_WIDTH] = ya.astype(BF16)

    xb_s[POOL_HALO:POOL_HALO + tile, :] = z_s[:, E_XB:E_XB + B_WIDTH]
    t_abs = t_abs0 + lax.broadcasted_iota(jnp.int32, (tile, 1), 0)
    for g, w in enumerate(POOL_WINDOWS):
        win = xb_s[:, LANES * g:LANES * (g + 1)]
        acc = win
        k = 1
        while k < w:
            acc = acc + pltpu.roll(acc, k, axis=0)
            k *= 2
        cnt = jnp.minimum(t_abs + 1, w).astype(F32)
        pooled = acc[POOL_HALO:, :] * (1.0 / cnt)
        diff = pooled - win[POOL_HALO:, :]
        yb = _dot(diff.astype(BF16), bw_ref[g]) * bscale_ref[:, LANES * g:LANES * (g + 1)]
        yb = yb * _silu(z_s[:, E_GB + LANES * g:E_GB + LANES * (g + 1)])
        y_s[:, A_WIDTH + LANES * g:A_WIDTH + LANES * (g + 1)] = yb.astype(BF16)

    _memory_attention(z_s, E_QM, E_GM, kt_ref, vm_ref, y_s, A_WIDTH + B_WIDTH)

    o_ref[...] = h_ref[...] + _dot(y_s[...], wout_ref[...])


def _sub_tiles(ref):
    n = ref.shape[0]
    return [(ref.at[k], ref.at[(k - 1) % n]) for k in range(n)]


def _even_kernel(h_ref, g_ref, win_ref, wout_ref, lng_ref, lnb_ref, w2_ref, abias_ref, bw_ref,
                 bscale_ref, kt_ref, vm_ref, o_ref, xn_s, z_s, y_s, sg_s, xb_s):
    t = pl.program_id(1)
    nsub, sub = xn_s.shape[0], xn_s.shape[1]

    @pl.when(t == 0)
    def _():
        xb_s[nsub - 1, sub:sub + POOL_HALO, :] = jnp.zeros((POOL_HALO, B_WIDTH), F32)

    for k, (xb, xb_prev) in enumerate(_sub_tiles(xb_s)):
        rows = pl.ds(k * sub, sub)
        _even_tile(t * (nsub * sub) + k * sub, h_ref.at[rows], g_ref, win_ref, wout_ref, lng_ref, lnb_ref,
                   w2_ref, abias_ref, bw_ref, bscale_ref, kt_ref, vm_ref, o_ref.at[rows],
                   xn_s.at[k], z_s.at[k], y_s.at[k], sg_s.at[k], xb, xb_prev)


def _odd_tile(first, sink_row, h_ref, g_ref, win_ref, wout_ref, qg_ref, kg_ref, sink_ref, dw_ref, dwb_ref, dlg_ref,
              dlb_ref, pw_ref, kt_ref, vm_ref, o_ref, xn_s, z_s, y_s, cv_s, hb_s, kb_s, vb_s,
              hb_prev, kb_prev, vb_prev):
    tile = h_ref.shape[0]
    nb = tile // BLOCK
    hb_s[0, 0:CONV_HALO, :] = hb_prev[0, tile:tile + CONV_HALO, :]
    kb_s[:, 0:BLOCK, :] = kb_prev[:, tile:tile + BLOCK, :]
    vb_s[:, 0:BLOCK, :] = vb_prev[:, tile:tile + BLOCK, :]

    halves = [slice(r, r + tile // 2) for r in (0, tile // 2)]
    for rows in halves:
        h = h_ref[rows, :]
        xn_s[rows, :] = (h * lax.rsqrt(jnp.mean(h * h, axis=-1, keepdims=True) + EPS) * g_ref[...]).astype(BF16)
        z_s[rows, O_DA:O_GD] = _dot(xn_s[rows, :], win_ref[:, O_DA:O_GD])
    for c0, c1 in ((O_GD, O_QM), (O_KC, O_GC), (O_QC, O_KC), (O_QM, ODD_IN), (O_GC, O_DA)):
        for rows in halves:
            z_s[rows, c0:c1] = _dot(xn_s[rows, :], win_ref[:, c0:c1])

    lo = _lo_lanes((tile, LANES))
    k = z_s[:, O_KC:O_KC + C_KV_WIDTH]
    kn = k * _pair_rs(k) * (kg_ref[...] * qg_ref[...] * (HEAD_DIM ** -0.5 * LOG2E))
    v = z_s[:, O_VC:O_VC + C_KV_WIDTH]
    for buf, x in ((kb_s, kn), (vb_s, v)):
        xsw = pltpu.roll(x, HEAD_DIM, axis=1)
        buf[0, BLOCK:BLOCK + tile, :] = jnp.where(lo, x, 0.0).astype(BF16)
        buf[1, BLOCK:BLOCK + tile, :] = jnp.where(lo, 0.0, x).astype(BF16)
        buf[2, BLOCK:BLOCK + tile, :] = jnp.where(lo, xsw, 0.0).astype(BF16)
        buf[3, BLOCK:BLOCK + tile, :] = jnp.where(lo, 0.0, xsw).astype(BF16)
    variant = ((0, 3), (2, 1))

    qi = lax.broadcasted_iota(jnp.int32, (BLOCK, 2 * BLOCK), 0)
    cj = lax.broadcasted_iota(jnp.int32, (BLOCK, 2 * BLOCK), 1)
    band = (cj - qi >= 1) & (cj - qi <= BLOCK)
    first_key = first * BLOCK
    neg = jnp.finfo(F32).min
    for j in range(nb):
        valid = band & (cj >= first_key) if j == 0 else band
        for s in range(C_Q_WIDTH // LANES):
            q = z_s[BLOCK * j:BLOCK * (j + 1), O_QC + LANES * s:O_QC + LANES * (s + 1)]
            qn = (q * _pair_rs(q)).astype(BF16)
            acc = None
            for pos in range(2):
                hd = 2 * s + pos
                var = variant[hd // 4][pos]
                sc = _dot_nt(qn, kb_s[var, BLOCK * j:BLOCK * (j + 2), :])
                sc = jnp.where(valid, sc, neg)
                sink = sink_ref[sink_row, hd] * LOG2E
                m = jnp.maximum(jnp.max(sc, axis=-1, keepdims=True), sink)
                e = jnp.exp2(sc - m)
                l = jnp.sum(e, axis=-1, keepdims=True) + jnp.exp2(sink - m)
                o = _dot(e.astype(BF16), vb_s[var, BLOCK * j:BLOCK * (j + 2), :]) * (1.0 / l)
                acc = o if acc is None else acc + o
            gate = _silu(z_s[BLOCK * j:BLOCK * (j + 1), O_GC + LANES * s:O_GC + LANES * (s + 1)])
            y_s[BLOCK * j:BLOCK * (j + 1), LANES * s:LANES * (s + 1)] = (acc * gate).astype(BF16)

    for rows in halves:
        _memory_attention(z_s, O_QM, O_GM, kt_ref, vm_ref, y_s, C_Q_WIDTH + D_WIDTH, rows)

    for rows in halves:
        hb_s[0, CONV_HALO + rows.start:CONV_HALO + rows.stop, :] = (
            z_s[rows, O_DA:O_DA + D_WIDTH] * _sigmoid(z_s[rows, O_DB:O_DB + D_WIDTH]))
    shifted_rows = CONV_HALO + tile - SUBLANES
    for s in range(D_WIDTH // LANES):
        x = hb_s[0, :, LANES * s:LANES * (s + 1)]
        for p in range(1, SUBLANES):
            hb_s[p, 0:shifted_rows, LANES * s:LANES * (s + 1)] = pltpu.roll(
                x, CONV_HALO + tile - p, axis=0)[0:shifted_rows, :]
    base = CONV_HALO - (CONV_WIDTH - 1)
    for s in range(D_WIDTH // LANES):
        for j in range(nb):
            acc = jnp.broadcast_to(dwb_ref[:, LANES * s:LANES * (s + 1)], (BLOCK, LANES))
            for tap in range(CONV_WIDTH):
                p, r = (base + tap) % SUBLANES, (base + tap) // SUBLANES * SUBLANES
                x = hb_s[p, BLOCK * j + r:BLOCK * (j + 1) + r, LANES * s:LANES * (s + 1)]
                acc = acc + x * dw_ref[tap:tap + 1, LANES * s:LANES * (s + 1)]
            cv_s[BLOCK * j:BLOCK * (j + 1), LANES * s:LANES * (s + 1)] = acc
    for rows in halves:
        hd_ = _silu(_layer_norm(cv_s[rows, :], dlg_ref[...], dlb_ref[...]))
        yd = _dot(hd_.astype(BF16), pw_ref[...]) * _silu(z_s[rows, O_GD:O_GD + D_WIDTH])
        y_s[rows, C_Q_WIDTH:C_Q_WIDTH + D_WIDTH] = yd.astype(BF16)

    for rows in halves:
        o_ref[rows, :] = h_ref[rows, :] + _dot(y_s[rows, :], wout_ref[...])


def _odd_kernel(sink_row, h_ref, g_ref, win_ref, wout_ref, qg_ref, kg_ref, sink_ref, dw_ref, dwb_ref, dlg_ref,
                dlb_ref, pw_ref, kt_ref, vm_ref, o_ref, xn_s, z_s, y_s, cv_s, hb_s, kb_s, vb_s):
    t = pl.program_id(1)
    nsub, sub = xn_s.shape[0], xn_s.shape[1]

    @pl.when(t == 0)
    def _():
        hb_s[nsub - 1, 0, sub:sub + CONV_HALO, :] = jnp.zeros((CONV_HALO, D_WIDTH), F32)
        kb_s[nsub - 1, :, sub:sub + BLOCK, :] = jnp.zeros((4, BLOCK, LANES), BF16)
        vb_s[nsub - 1, :, sub:sub + BLOCK, :] = jnp.zeros((4, BLOCK, LANES), BF16)

    halos = zip(_sub_tiles(hb_s), _sub_tiles(kb_s), _sub_tiles(vb_s))
    for k, ((hb, hb_prev), (kb, kb_prev), (vb, vb_prev)) in enumerate(halos):
        rows = pl.ds(k * sub, sub)
        first = (t == 0).astype(jnp.int32) if k == 0 else 0
        _odd_tile(first, sink_row, h_ref.at[rows], g_ref, win_ref, wout_ref, qg_ref, kg_ref, sink_ref, dw_ref,
                  dwb_ref, dlg_ref, dlb_ref, pw_ref, kt_ref, vm_ref, o_ref.at[rows], xn_s.at[k], z_s.at[k],
                  y_s.at[k], cv_s.at[k], hb, kb, vb, hb_prev, kb_prev, vb_prev)


def _mem_prep_kernel(mem_ref, g_ref, w_ref, qn_ref, kn_ref, kt_ref, vm_ref):
    m = mem_ref[...]
    mn = (m * lax.rsqrt(jnp.mean(m * m, axis=-1, keepdims=True) + EPS) * g_ref[...]).astype(BF16)
    zeros = jnp.zeros((HEAD_DIM, N_MEM), F32)
    lo = _lo_lanes((N_MEM, LANES))
    for layer in range(w_ref.shape[0]):
        kv = _dot(mn, w_ref[layer])
        kt = kv[:, 0:M_WIDTH].T
        gain = kn_ref[layer] * qn_ref[layer] * (HEAD_DIM ** -0.5 * LOG2E)
        for hd in range(M_HEADS):
            kh = kt[HEAD_DIM * hd:HEAD_DIM * (hd + 1), :]
            kh = kh * lax.rsqrt(jnp.mean(kh * kh, axis=0, keepdims=True) + EPS) * gain
            parts = [kh, zeros] if hd % 2 == 0 else [zeros, kh]
            kt_ref[layer, hd] = jnp.concatenate(parts, axis=0).astype(BF16)
            s = hd // 2
            v = kv[:, M_WIDTH + LANES * s:M_WIDTH + LANES * (s + 1)]
            vm_ref[layer, hd] = (jnp.where(lo, v, 0.0) if hd % 2 == 0 else jnp.where(lo, 0.0, v)).astype(BF16)


def _mem_prep(mem, mem_norm_g, w_mem_kv, m_qnorm, m_knorm):
    depth, batch = w_mem_kv.shape[0], mem.shape[0]
    whole = lambda shape: pl.BlockSpec(shape, lambda b: (0,) * len(shape))
    return pl.pallas_call(
        _mem_prep_kernel,
        grid=(batch,),
        in_specs=[
            pl.BlockSpec((None, N_MEM, D_MODEL), lambda b: (b, 0, 0)),
            whole((1, D_MODEL)),
            whole((depth, D_MODEL, 2 * M_WIDTH)),
            whole((depth, HEAD_DIM, 1)),
            whole((depth, HEAD_DIM, 1)),
        ],
        out_specs=[
            pl.BlockSpec((depth, None, M_HEADS, LANES, N_MEM), lambda b: (0, b, 0, 0, 0)),
            pl.BlockSpec((depth, None, M_HEADS, N_MEM, LANES), lambda b: (0, b, 0, 0, 0)),
        ],
        out_shape=[
            jax.ShapeDtypeStruct((depth, batch, M_HEADS, LANES, N_MEM), BF16),
            jax.ShapeDtypeStruct((depth, batch, M_HEADS, N_MEM, LANES), BF16),
        ],
        compiler_params=pltpu.CompilerParams(dimension_semantics=("arbitrary",)),
        name="mem_prep",
    )(mem, mem_norm_g.reshape(1, D_MODEL), w_mem_kv.astype(BF16),
      m_qnorm.reshape(depth, HEAD_DIM, 1), m_knorm.reshape(depth, HEAD_DIM, 1))


def _layer_of(stacked, index):
    tail = stacked.shape[1:]
    return pl.BlockSpec((None,) + tail, lambda b, t: (index,) + (0,) * len(tail), pipeline_mode=pl.Buffered(1))


def _rows(x):
    return x.reshape(x.shape[0], 1, x.shape[1])


def _layer_call(body, name, h, in_width, tile, nsub, scratch, operands, layer, kt, vm):
    batch, seq, _ = h.shape
    sub = tile // nsub
    h_spec = pl.BlockSpec((None, tile, D_MODEL), lambda b, t: (b, t, 0))
    mem_specs = [
        pl.BlockSpec((None, None, M_HEADS, LANES, N_MEM), lambda b, t: (layer, b, 0, 0, 0)),
        pl.BlockSpec((None, None, M_HEADS, N_MEM, LANES), lambda b, t: (layer, b, 0, 0, 0)),
    ]
    return pl.pallas_call(
        body,
        grid=(batch, seq // tile),
        in_specs=[h_spec] + [spec for _, spec in operands] + mem_specs,
        out_specs=h_spec,
        out_shape=jax.ShapeDtypeStruct(h.shape, h.dtype),
        scratch_shapes=[
            pltpu.VMEM((nsub, sub, D_MODEL), BF16),
            pltpu.VMEM((nsub, sub, in_width), F32),
            pltpu.VMEM((nsub, sub, OUT_WIDTH), BF16),
        ] + scratch,
        compiler_params=pltpu.CompilerParams(
            dimension_semantics=("arbitrary", "arbitrary"), vmem_limit_bytes=VMEM_LIMIT),
        name=name,
    )(h, *[x for x, _ in operands], kt, vm)


def kernel(x, mem, norm_g, mem_norm_g, w_mem_kv, m_qnorm, m_knorm, w_in_even, w_out_even, a_ln_g, a_ln_b, a_ws, a_bs, b_w, b_scale, w_in_odd, w_out_odd, c_qnorm, c_knorm, c_sink, d_dw, d_dw_b, d_ln_g, d_ln_b, d_pw):
    kt, vm = _mem_prep(mem, mem_norm_g, w_mem_kv, m_qnorm, m_knorm)

    n_even, heads = a_ws.shape[0], a_ws.shape[1]
    norm_rows = _rows(norm_g)
    even = dict(
        w_in=w_in_even.astype(BF16), w_out=w_out_even.astype(BF16), ln_g=_rows(a_ln_g), ln_b=_rows(a_ln_b),
        w2=a_ws.reshape(n_even, heads // 2, 2, CHUNK, CHUNK).transpose(0, 1, 3, 2, 4)
               .reshape(n_even, heads // 2, CHUNK, 2 * CHUNK).astype(BF16),
        abias=jnp.repeat(a_bs.reshape(n_even, heads // 2, 2, CHUNK).transpose(0, 1, 3, 2), HEAD_DIM, axis=3),
        bw=b_w.astype(BF16), bscale=_rows(b_scale))
    odd = dict(
        w_in=w_in_odd.astype(BF16), w_out=w_out_odd.astype(BF16),
        qg=_rows(jnp.tile(c_qnorm, (1, 2))), kg=_rows(jnp.tile(c_knorm, (1, 2))),
        dw=d_dw, dwb=_rows(d_dw_b), ln_g=_rows(d_ln_g), ln_b=_rows(d_ln_b), pw=d_pw.astype(BF16))
    even_scratch = [pltpu.VMEM((EVEN_SUB_TILES, EVEN_TILE // EVEN_SUB_TILES, A_WIDTH), F32),
                    pltpu.VMEM((EVEN_SUB_TILES, POOL_HALO + EVEN_TILE // EVEN_SUB_TILES, B_WIDTH), F32)]
    odd_sub = TILE // ODD_SUB_TILES
    odd_scratch = [pltpu.VMEM((ODD_SUB_TILES, odd_sub, D_WIDTH), F32),
                   pltpu.VMEM((ODD_SUB_TILES, SUBLANES, CONV_HALO + odd_sub, D_WIDTH), F32),
                   pltpu.VMEM((ODD_SUB_TILES, 4, BLOCK + odd_sub, LANES), BF16),
                   pltpu.VMEM((ODD_SUB_TILES, 4, BLOCK + odd_sub, LANES), BF16)]

    h = x
    for layer in range(w_mem_kv.shape[0]):
        i = layer // 2
        norm = (norm_rows, _layer_of(norm_rows, layer))
        if layer % 2 == 0:
            names = ("w_in", "w_out", "ln_g", "ln_b", "w2", "abias", "bw", "bscale")
            operands = [norm] + [(even[n], _layer_of(even[n], i)) for n in names]
            h = _layer_call(_even_kernel, "even_layer", h, EVEN_IN, EVEN_TILE, EVEN_SUB_TILES, even_scratch, operands,
                            layer, kt, vm)
        else:
            operands = [norm] + [(odd[n], _layer_of(odd[n], i)) for n in ("w_in", "w_out", "qg", "kg")]
            operands.append((c_sink, pl.BlockSpec(memory_space=pltpu.SMEM)))
            operands += [(odd[n], _layer_of(odd[n], i)) for n in ("dw", "dwb", "ln_g", "ln_b", "pw")]
            h = _layer_call(functools.partial(_odd_kernel, i), "odd_layer", h, ODD_IN, TILE, ODD_SUB_TILES,
                            odd_scratch, operands, layer, kt, vm)
    return h
```

```python
import functools

import jax
import jax.numpy as jnp
from jax import lax
from jax.experimental import pallas as pl
from jax.experimental.pallas import tpu as pltpu

D_MODEL = 1024
N_MEM = 256
HEAD_DIM = 64
EPS = 1e-6
LOG2E = 1.4426950408889634
A_WIDTH = 512
CHUNK = 128
POOL_WINDOWS = (2, 4, 8, 16)
B_WIDTH = 512
C_Q_WIDTH = 512
C_KV_WIDTH = 128
BLOCK = 128
D_WIDTH = 512
CONV_WIDTH = 31
M_HEADS = 4
M_WIDTH = 256
OUT_WIDTH = 1280
EVEN_IN = 3072
ODD_IN = 3328

LANES = 128
SUBLANES = 8
POOL_HALO = 16
CONV_HALO = 32
TILE = 512
EVEN_TILE = 1024
EVEN_SUB_TILES = 1
ODD_SUB_TILES = 1
VMEM_LIMIT = 56 * 1024 * 1024

F32 = jnp.float32
BF16 = jnp.bfloat16

E_U, E_V, E_GA, E_XB, E_GB, E_QM, E_GM = 0, 512, 1024, 1536, 2048, 2560, 2816
O_QC, O_KC, O_VC, O_GC, O_DA, O_DB, O_GD, O_QM, O_GM = 0, 512, 640, 768, 1280, 1792, 2304, 2816, 3072


def _dot(a, b):
    return jnp.dot(a, b, preferred_element_type=F32)


def _dot_nt(a, b):
    return lax.dot_general(a, b, (((1,), (1,)), ((), ())), preferred_element_type=F32)


def _sigmoid(x):
    return 0.5 * jnp.tanh(0.5 * x) + 0.5


def _silu(x):
    hx = 0.5 * x
    return hx * jnp.tanh(hx) + hx


def _lo_lanes(shape):
    return lax.broadcasted_iota(jnp.int32, shape, len(shape) - 1) < HEAD_DIM


def _pair_rs(x):
    lo = _lo_lanes(x.shape)
    x2 = x * x
    s_lo = jnp.sum(jnp.where(lo, x2, 0.0), axis=-1, keepdims=True)
    s_hi = jnp.sum(jnp.where(lo, 0.0, x2), axis=-1, keepdims=True)
    return lax.rsqrt(jnp.where(lo, s_lo, s_hi) * (1.0 / HEAD_DIM) + EPS)


def _layer_norm(x, g, b):
    mu = jnp.mean(x, axis=-1, keepdims=True)
    xc = x - mu
    return xc * lax.rsqrt(jnp.mean(xc * xc, axis=-1, keepdims=True) + EPS) * g + b


def _project_in(h_ref, g_ref, win_ref, xn_s, z_s, groups):
    h = h_ref[...]
    xn = h * lax.rsqrt(jnp.mean(h * h, axis=-1, keepdims=True) + EPS) * g_ref[...]
    xn_s[...] = xn.astype(BF16)
    for lo, hi in groups:
        z_s[:, lo:hi] = _dot(xn_s[...], win_ref[:, lo:hi])


def _memory_attention(z_s, q_off, g_off, kt_ref, vm_ref, y_s, y_off, rows=slice(None)):
    for s in range(M_WIDTH // LANES):
        q = z_s[rows, q_off + LANES * s:q_off + LANES * (s + 1)]
        qn = (q * _pair_rs(q)).astype(BF16)
        acc = None
        for pos in range(2):
            hd = 2 * s + pos
            sc = _dot(qn, kt_ref[hd])
            m = jnp.max(sc, axis=-1, keepdims=True)
            e = jnp.exp2(sc - m)
            l = jnp.sum(e, axis=-1, keepdims=True)
            o = _dot(e.astype(BF16), vm_ref[hd]) * (1.0 / l)
            acc = o if acc is None else acc + o
        gate = _silu(z_s[rows, g_off + LANES * s:g_off + LANES * (s + 1)])
        y_s[rows, y_off + LANES * s:y_off + LANES * (s + 1)] = (acc * gate).astype(BF16)


def _even_tile(t_abs0, h_ref, g_ref, win_ref, wout_ref, lng_ref, lnb_ref, w2_ref, abias_ref, bw_ref,
               bscale_ref, kt_ref, vm_ref, o_ref, xn_s, z_s, y_s, sg_s, xb_s, xb_prev):
    tile = h_ref.shape[0]
    nc = tile // CHUNK
    xb_s[0:POOL_HALO, :] = xb_prev[tile:tile + POOL_HALO, :]

    _project_in(h_ref, g_ref, win_ref, xn_s, z_s, ((E_XB, E_QM), (E_U, E_XB), (E_QM, EVEN_IN)))

    vn = _layer_norm(z_s[:, E_V:E_V + A_WIDTH], lng_ref[...], lnb_ref[...])
    lo = _lo_lanes((CHUNK, LANES))
    row = lax.broadcasted_iota(jnp.int32, (CHUNK, 2 * CHUNK), 0)
    col = lax.broadcasted_iota(jnp.int32, (CHUNK, 2 * CHUNK), 1)
    tril2 = (col & (CHUNK - 1)) <= row
    for s in range(A_WIDTH // LANES):
        tops, bots = [], []
        for c in range(nc):
            blk = vn[CHUNK * c:CHUNK * (c + 1), LANES * s:LANES * (s + 1)]
            tops.append(jnp.where(lo, blk, 0.0).astype(BF16))
            bots.append(jnp.where(lo, 0.0, blk).astype(BF16))
        rhs = jnp.concatenate([jnp.concatenate(tops, axis=1), jnp.concatenate(bots, axis=1)], axis=0)
        w2 = jnp.where(tril2, w2_ref[s], jnp.zeros((), BF16))
        sg = _dot(w2, rhs)
        for c in range(nc):
            sg_s[CHUNK * c:CHUNK * (c + 1), LANES * s:LANES * (s + 1)] = (
                sg[:, LANES * c:LANES * (c + 1)] + abias_ref[s])
    ya = z_s[:, E_U:E_U + A_WIDTH] * sg_s[...] * _silu(z_s[:, E_GA:E_GA + A_WIDTH])
    y_s[:, 0:A_WIDTH] = ya.astype(BF16)

    xb_s[POOL_HALO:POOL_HALO + tile, :] = z_s[:, E_XB:E_XB + B_WIDTH]
    t_abs = t_abs0 + lax.broadcasted_iota(jnp.int32, (tile, 1), 0)
    for g, w in enumerate(POOL_WINDOWS):
        win = xb_s[:, LANES * g:LANES * (g + 1)]
        acc = win
        k = 1
        while k < w:
            acc = acc + pltpu.roll(acc, k, axis=0)
            k *= 2
        cnt = jnp.minimum(t_abs + 1, w).astype(F32)
        pooled = acc[POOL_HALO:, :] * (1.0 / cnt)
        diff = pooled - win[POOL_HALO:, :]
        yb = _dot(diff.astype(BF16), bw_ref[g]) * bscale_ref[:, LANES * g:LANES * (g + 1)]
        yb = yb * _silu(z_s[:, E_GB + LANES * g:E_GB + LANES * (g + 1)])
        y_s[:, A_WIDTH + LANES * g:A_WIDTH + LANES * (g + 1)] = yb.astype(BF16)

    _memory_attention(z_s, E_QM, E_GM, kt_ref, vm_ref, y_s, A_WIDTH + B_WIDTH)

    o_ref[...] = h_ref[...] + _dot(y_s[...], wout_ref[...])


def _sub_tiles(ref):
    n = ref.shape[0]
    return [(ref.at[k], ref.at[(k - 1) % n]) for k in range(n)]


def _even_kernel(h_ref, g_ref, win_ref, wout_ref, lng_ref, lnb_ref, w2_ref, abias_ref, bw_ref,
                 bscale_ref, kt_ref, vm_ref, o_ref, xn_s, z_s, y_s, sg_s, xb_s):
    t = pl.program_id(1)
    nsub, sub = xn_s.shape[0], xn_s.shape[1]

    @pl.when(t == 0)
    def _():
        xb_s[nsub - 1, sub:sub + POOL_HALO, :] = jnp.zeros((POOL_HALO, B_WIDTH), F32)

    for k, (xb, xb_prev) in enumerate(_sub_tiles(xb_s)):
        rows = pl.ds(k * sub, sub)
        _even_tile(t * (nsub * sub) + k * sub, h_ref.at[rows], g_ref, win_ref, wout_ref, lng_ref, lnb_ref,
                   w2_ref, abias_ref, bw_ref, bscale_ref, kt_ref, vm_ref, o_ref.at[rows],
                   xn_s.at[k], z_s.at[k], y_s.at[k], sg_s.at[k], xb, xb_prev)


def _odd_tile(first, sink_row, h_ref, g_ref, win_ref, wout_ref, qg_ref, kg_ref, sink_ref, dw_ref, dwb_ref, dlg_ref,
              dlb_ref, pw_ref, kt_ref, vm_ref, o_ref, xn_s, z_s, y_s, cv_s, hb_s, kb_s, vb_s,
              hb_prev, kb_prev, vb_prev):
    tile = h_ref.shape[0]
    nb = tile // BLOCK
    hb_s[0, 0:CONV_HALO, :] = hb_prev[0, tile:tile + CONV_HALO, :]
    kb_s[:, 0:BLOCK, :] = kb_prev[:, tile:tile + BLOCK, :]
    vb_s[:, 0:BLOCK, :] = vb_prev[:, tile:tile + BLOCK, :]

    halves = [slice(r, r + tile // 2) for r in (0, tile // 2)]
    for rows in halves:
        h = h_ref[rows, :]
        xn_s[rows, :] = (h * lax.rsqrt(jnp.mean(h * h, axis=-1, keepdims=True) + EPS) * g_ref[...]).astype(BF16)
        z_s[rows, O_DA:O_GD] = _dot(xn_s[rows, :], win_ref[:, O_DA:O_GD])
    for c0, c1 in ((O_GD, O_QM), (O_KC, O_GC), (O_QC, O_KC), (O_QM, ODD_IN), (O_GC, O_DA)):
        for rows in halves:
            z_s[rows, c0:c1] = _dot(xn_s[rows, :], win_ref[:, c0:c1])

    lo = _lo_lanes((tile, LANES))
    k = z_s[:, O_KC:O_KC + C_KV_WIDTH]
    kn = k * _pair_rs(k) * (kg_ref[...] * qg_ref[...] * (HEAD_DIM ** -0.5 * LOG2E))
    v = z_s[:, O_VC:O_VC + C_KV_WIDTH]
    for buf, x in ((kb_s, kn), (vb_s, v)):
        xsw = pltpu.roll(x, HEAD_DIM, axis=1)
        buf[0, BLOCK:BLOCK + tile, :] = jnp.where(lo, x, 0.0).astype(BF16)
        buf[1, BLOCK:BLOCK + tile, :] = jnp.where(lo, 0.0, x).astype(BF16)
        buf[2, BLOCK:BLOCK + tile, :] = jnp.where(lo, xsw, 0.0).astype(BF16)
        buf[3, BLOCK:BLOCK + tile, :] = jnp.where(lo, 0.0, xsw).astype(BF16)
    variant = ((0, 3), (2, 1))

    qi = lax.broadcasted_iota(jnp.int32, (BLOCK, 2 * BLOCK), 0)
    cj = lax.broadcasted_iota(jnp.int32, (BLOCK, 2 * BLOCK), 1)
    band = (cj - qi >= 1) & (cj - qi <= BLOCK)
    first_key = first * BLOCK
    neg = jnp.finfo(F32).min
    for j in range(nb):
        valid = band & (cj >= first_key) if j == 0 else band
        for s in range(C_Q_WIDTH // LANES):
            q = z_s[BLOCK * j:BLOCK * (j + 1), O_QC + LANES * s:O_QC + LANES * (s + 1)]
            qn = (q * _pair_rs(q)).astype(BF16)
            acc = None
            for pos in range(2):
                hd = 2 * s + pos
                var = variant[hd // 4][pos]
                sc = _dot_nt(qn, kb_s[var, BLOCK * j:BLOCK * (j + 2), :])
                sc = jnp.where(valid, sc, neg)
                sink = sink_ref[sink_row, hd] * LOG2E
                m = jnp.maximum(jnp.max(sc, axis=-1, keepdims=True), sink)
                e = jnp.exp2(sc - m)
                l = jnp.sum(e, axis=-1, keepdims=True) + jnp.exp2(sink - m)
                o = _dot(e.astype(BF16), vb_s[var, BLOCK * j:BLOCK * (j + 2), :]) * (1.0 / l)
                acc = o if acc is None else acc + o
            gate = _silu(z_s[BLOCK * j:BLOCK * (j + 1), O_GC + LANES * s:O_GC + LANES * (s + 1)])
            y_s[BLOCK * j:BLOCK * (j + 1), LANES * s:LANES * (s + 1)] = (acc * gate).astype(BF16)

    for rows in halves:
        _memory_attention(z_s, O_QM, O_GM, kt_ref, vm_ref, y_s, C_Q_WIDTH + D_WIDTH, rows)

    for rows in halves:
        hb_s[0, CONV_HALO + rows.start:CONV_HALO + rows.stop, :] = (
            z_s[rows, O_DA:O_DA + D_WIDTH] * _sigmoid(z_s[rows, O_DB:O_DB + D_WIDTH]))
    shifted_rows = CONV_HALO + tile - SUBLANES
    for s in range(D_WIDTH // LANES):
        x = hb_s[0, :, LANES * s:LANES * (s + 1)]
        for p in range(1, SUBLANES):
            hb_s[p, 0:shifted_rows, LANES * s:LANES * (s + 1)] = pltpu.roll(
                x, CONV_HALO + tile - p, axis=0)[0:shifted_rows, :]
    base = CONV_HALO - (CONV_WIDTH - 1)
    for s in range(D_WIDTH // LANES):
        for j in range(nb):
            acc = jnp.broadcast_to(dwb_ref[:, LANES * s:LANES * (s + 1)], (BLOCK, LANES))
            for tap in range(CONV_WIDTH):
                p, r = (base + tap) % SUBLANES, (base + tap) // SUBLANES * SUBLANES
                x = hb_s[p, BLOCK * j + r:BLOCK * (j + 1) + r, LANES * s:LANES * (s + 1)]
                acc = acc + x * dw_ref[tap:tap + 1, LANES * s:LANES * (s + 1)]
            cv_s[BLOCK * j:BLOCK * (j + 1), LANES * s:LANES * (s + 1)] = acc
    for rows in halves:
        hd_ = _silu(_layer_norm(cv_s[rows, :], dlg_ref[...], dlb_ref[...]))
        yd = _dot(hd_.astype(BF16), pw_ref[...]) * _silu(z_s[rows, O_GD:O_GD + D_WIDTH])
        y_s[rows, C_Q_WIDTH:C_Q_WIDTH + D_WIDTH] = yd.astype(BF16)

    for rows in halves:
        o_ref[rows, :] = h_ref[rows, :] + _dot(y_s[rows, :], wout_ref[...])


def _odd_kernel(sink_row, h_ref, g_ref, win_ref, wout_ref, qg_ref, kg_ref, sink_ref, dw_ref, dwb_ref, dlg_ref,
                dlb_ref, pw_ref, kt_ref, vm_ref, o_ref, xn_s, z_s, y_s, cv_s, hb_s, kb_s, vb_s):
    t = pl.program_id(1)
    nsub, sub = xn_s.shape[0], xn_s.shape[1]

    @pl.when(t == 0)
    def _():
        hb_s[nsub - 1, 0, sub:sub + CONV_HALO, :] = jnp.zeros((CONV_HALO, D_WIDTH), F32)
        kb_s[nsub - 1, :, sub:sub + BLOCK, :] = jnp.zeros((4, BLOCK, LANES), BF16)
        vb_s[nsub - 1, :, sub:sub + BLOCK, :] = jnp.zeros((4, BLOCK, LANES), BF16)

    halos = zip(_sub_tiles(hb_s), _sub_tiles(kb_s), _sub_tiles(vb_s))
    for k, ((hb, hb_prev), (kb, kb_prev), (vb, vb_prev)) in enumerate(halos):
        rows = pl.ds(k * sub, sub)
        first = (t == 0).astype(jnp.int32) if k == 0 else 0
        _odd_tile(first, sink_row, h_ref.at[rows], g_ref, win_ref, wout_ref, qg_ref, kg_ref, sink_ref, dw_ref,
                  dwb_ref, dlg_ref, dlb_ref, pw_ref, kt_ref, vm_ref, o_ref.at[rows], xn_s.at[k], z_s.at[k],
                  y_s.at[k], cv_s.at[k], hb, kb, vb, hb_prev, kb_prev, vb_prev)


def _mem_prep_kernel(mem_ref, g_ref, w_ref, qn_ref, kn_ref, kt_ref, vm_ref):
    m = mem_ref[...]
    mn = (m * lax.rsqrt(jnp.mean(m * m, axis=-1, keepdims=True) + EPS) * g_ref[...]).astype(BF16)
    zeros = jnp.zeros((HEAD_DIM, N_MEM), F32)
    lo = _lo_lanes((N_MEM, LANES))
    for layer in range(w_ref.shape[0]):
        kv = _dot(mn, w_ref[layer])
        kt = kv[:, 0:M_WIDTH].T
        gain = kn_ref[layer] * qn_ref[layer] * (HEAD_DIM ** -0.5 * LOG2E)
        for hd in range(M_HEADS):
            kh = kt[HEAD_DIM * hd:HEAD_DIM * (hd + 1), :]
            kh = kh * lax.rsqrt(jnp.mean(kh * kh, axis=0, keepdims=True) + EPS) * gain
            parts = [kh, zeros] if hd % 2 == 0 else [zeros, kh]
            kt_ref[layer, hd] = jnp.concatenate(parts, axis=0).astype(BF16)
            s = hd // 2
            v = kv[:, M_WIDTH + LANES * s:M_WIDTH + LANES * (s + 1)]
            vm_ref[layer, hd] = (jnp.where(lo, v, 0.0) if hd % 2 == 0 else jnp.where(lo, 0.0, v)).astype(BF16)


def _mem_prep(mem, mem_norm_g, w_mem_kv, m_qnorm, m_knorm):
    depth, batch = w_mem_kv.shape[0], mem.shape[0]
    whole = lambda shape: pl.BlockSpec(shape, lambda b: (0,) * len(shape))
    return pl.pallas_call(
        _mem_prep_kernel,
        grid=(batch,),
        in_specs=[
            pl.BlockSpec((None, N_MEM, D_MODEL), lambda b: (b, 0, 0)),
            whole((1, D_MODEL)),
            whole((depth, D_MODEL, 2 * M_WIDTH)),
            whole((depth, HEAD_DIM, 1)),
            whole((depth, HEAD_DIM, 1)),
        ],
        out_specs=[
            pl.BlockSpec((depth, None, M_HEADS, LANES, N_MEM), lambda b: (0, b, 0, 0, 0)),
            pl.BlockSpec((depth, None, M_HEADS, N_MEM, LANES), lambda b: (0, b, 0, 0, 0)),
        ],
        out_shape=[
            jax.ShapeDtypeStruct((depth, batch, M_HEADS, LANES, N_MEM), BF16),
            jax.ShapeDtypeStruct((depth, batch, M_HEADS, N_MEM, LANES), BF16),
        ],
        compiler_params=pltpu.CompilerParams(dimension_semantics=("arbitrary",)),
        name="mem_prep",
    )(mem, mem_norm_g.reshape(1, D_MODEL), w_mem_kv.astype(BF16),
      m_qnorm.reshape(depth, HEAD_DIM, 1), m_knorm.reshape(depth, HEAD_DIM, 1))


def _layer_of(stacked, index):
    tail = stacked.shape[1:]
    return pl.BlockSpec((None,) + tail, lambda b, t: (index,) + (0,) * len(tail), pipeline_mode=pl.Buffered(1))


def _rows(x):
    return x.reshape(x.shape[0], 1, x.shape[1])


def _layer_call(body, name, h, in_width, tile, nsub, scratch, operands, layer, kt, vm):
    batch, seq, _ = h.shape
    sub = tile // nsub
    h_spec = pl.BlockSpec((None, tile, D_MODEL), lambda b, t: (b, t, 0))
    mem_specs = [
        pl.BlockSpec((None, None, M_HEADS, LANES, N_MEM), lambda b, t: (layer, b, 0, 0, 0)),
        pl.BlockSpec((None, None, M_HEADS, N_MEM, LANES), lambda b, t: (layer, b, 0, 0, 0)),
    ]
    return pl.pallas_call(
        body,
        grid=(batch, seq // tile),
        in_specs=[h_spec] + [spec for _, spec in operands] + mem_specs,
        out_specs=h_spec,
        out_shape=jax.ShapeDtypeStruct(h.shape, h.dtype),
        scratch_shapes=[
            pltpu.VMEM((nsub, sub, D_MODEL), BF16),
            pltpu.VMEM((nsub, sub, in_width), F32),
            pltpu.VMEM((nsub, sub, OUT_WIDTH), BF16),
        ] + scratch,
        compiler_params=pltpu.CompilerParams(
            dimension_semantics=("arbitrary", "arbitrary"), vmem_limit_bytes=VMEM_LIMIT),
        name=name,
    )(h, *[x for x, _ in operands], kt, vm)


def kernel(x, mem, norm_g, mem_norm_g, w_mem_kv, m_qnorm, m_knorm, w_in_even, w_out_even, a_ln_g, a_ln_b, a_ws, a_bs, b_w, b_scale, w_in_odd, w_out_odd, c_qnorm, c_knorm, c_sink, d_dw, d_dw_b, d_ln_g, d_ln_b, d_pw):
    kt, vm = _mem_prep(mem, mem_norm_g, w_mem_kv, m_qnorm, m_knorm)

    n_even, heads = a_ws.shape[0], a_ws.shape[1]
    norm_rows = _rows(norm_g)
    even = dict(
        w_in=w_in_even.astype(BF16), w_out=w_out_even.astype(BF16), ln_g=_rows(a_ln_g), ln_b=_rows(a_ln_b),
        w2=a_ws.reshape(n_even, heads // 2, 2, CHUNK, CHUNK).transpose(0, 1, 3, 2, 4)
               .reshape(n_even, heads // 2, CHUNK, 2 * CHUNK).astype(BF16),
        abias=jnp.repeat(a_bs.reshape(n_even, heads // 2, 2, CHUNK).transpose(0, 1, 3, 2), HEAD_DIM, axis=3),
        bw=b_w.astype(BF16), bscale=_rows(b_scale))
    odd = dict(
        w_in=w_in_odd.astype(BF16), w_out=w_out_odd.astype(BF16),
        qg=_rows(jnp.tile(c_qnorm, (1, 2))), kg=_rows(jnp.tile(c_knorm, (1, 2))),
        dw=d_dw, dwb=_rows(d_dw_b), ln_g=_rows(d_ln_g), ln_b=_rows(d_ln_b), pw=d_pw.astype(BF16))
    even_scratch = [pltpu.VMEM((EVEN_SUB_TILES, EVEN_TILE // EVEN_SUB_TILES, A_WIDTH), F32),
                    pltpu.VMEM((EVEN_SUB_TILES, POOL_HALO + EVEN_TILE // EVEN_SUB_TILES, B_WIDTH), F32)]
    odd_sub = TILE // ODD_SUB_TILES
    odd_scratch = [pltpu.VMEM((ODD_SUB_TILES, odd_sub, D_WIDTH), F32),
                   pltpu.VMEM((ODD_SUB_TILES, SUBLANES, CONV_HALO + odd_sub, D_WIDTH), F32),
                   pltpu.VMEM((ODD_SUB_TILES, 4, BLOCK + odd_sub, LANES), BF16),
                   pltpu.VMEM((ODD_SUB_TILES, 4, BLOCK + odd_sub, LANES), BF16)]

    h = x
    for layer in range(w_mem_kv.shape[0]):
        i = layer // 2
        norm = (norm_rows, _layer_of(norm_rows, layer))
        if layer % 2 == 0:
            names = ("w_in", "w_out", "ln_g", "ln_b", "w2", "abias", "bw", "bscale")
            operands = [norm] + [(even[n], _layer_of(even[n], i)) for n in names]
            h = _layer_call(_even_kernel, "even_layer", h, EVEN_IN, EVEN_TILE, EVEN_SUB_TILES, even_scratch, operands,
                            layer, kt, vm)
        else:
            operands = [norm] + [(odd[n], _layer_of(odd[n], i)) for n in ("w_in", "w_out", "qg", "kg")]
            operands.append((c_sink, pl.BlockSpec(memory_space=pltpu.SMEM)))
            operands += [(odd[n], _layer_of(odd[n], i)) for n in ("dw", "dwb", "ln_g", "ln_b", "pw")]
            h = _layer_call(functools.partial(_odd_kernel, i), "odd_layer", h, ODD_IN, TILE, ODD_SUB_TILES,
                            odd_scratch, operands, layer, kt, vm)
    return h
```

```python
import functools

import jax
import jax.numpy as jnp
from jax import lax
from jax.experimental import pallas as pl
from jax.experimental.pallas import tpu as pltpu

D_MODEL = 1024
N_MEM = 256
HEAD_DIM = 64
EPS = 1e-6
LOG2E = 1.4426950408889634
A_WIDTH = 512
CHUNK = 128
POOL_WINDOWS = (2, 4, 8, 16)
B_WIDTH = 512
C_Q_WIDTH = 512
C_KV_WIDTH = 128
BLOCK = 128
D_WIDTH = 512
CONV_WIDTH = 31
M_HEADS = 4
M_WIDTH = 256
OUT_WIDTH = 1280
EVEN_IN = 3072
ODD_IN = 3328

LANES = 128
SUBLANES = 8
POOL_HALO = 16
CONV_HALO = 32
TILE = 512
EVEN_TILE = 1024
EVEN_SUB_TILES = 2
ODD_SUB_TILES = 1
VMEM_LIMIT = 56 * 1024 * 1024

F32 = jnp.float32
BF16 = jnp.bfloat16

E_U, E_V, E_GA, E_XB, E_GB, E_QM, E_GM = 0, 512, 1024, 1536, 2048, 2560, 2816
O_QC, O_KC, O_VC, O_GC, O_DA, O_DB, O_GD, O_QM, O_GM = 0, 512, 640, 768, 1280, 1792, 2304, 2816, 3072


def _dot(a, b):
    return jnp.dot(a, b, preferred_element_type=F32)


def _dot_nt(a, b):
    return lax.dot_general(a, b, (((1,), (1,)), ((), ())), preferred_element_type=F32)


def _sigmoid(x):
    return 0.5 * jnp.tanh(0.5 * x) + 0.5


def _silu(x):
    hx = 0.5 * x
    return hx * jnp.tanh(hx) + hx


def _lo_lanes(shape):
    return lax.broadcasted_iota(jnp.int32, shape, len(shape) - 1) < HEAD_DIM


def _head_ones():
    r = lax.broadcasted_iota(jnp.int32, (LANES, LANES), 0) // HEAD_DIM
    c = lax.broadcasted_iota(jnp.int32, (LANES, LANES), 1) // HEAD_DIM
    return jnp.where(r == c, 1.0, 0.0).astype(BF16)


def _pair_rs(x, on_mxu=False):
    x2 = x * x
    if on_mxu:
        return lax.rsqrt(_dot(x2.astype(BF16), _head_ones()) * (1.0 / HEAD_DIM) + EPS)
    lo = _lo_lanes(x.shape)
    s_lo = jnp.sum(jnp.where(lo, x2, 0.0), axis=-1, keepdims=True)
    s_hi = jnp.sum(jnp.where(lo, 0.0, x2), axis=-1, keepdims=True)
    return lax.rsqrt(jnp.where(lo, s_lo, s_hi) * (1.0 / HEAD_DIM) + EPS)


def _layer_norm(x, g, b):
    mu = jnp.mean(x, axis=-1, keepdims=True)
    xc = x - mu
    return xc * lax.rsqrt(jnp.mean(xc * xc, axis=-1, keepdims=True) + EPS) * g + b


def _project_in(h_ref, g_ref, win_ref, xn_s, z_s, groups):
    h = h_ref[...]
    xn = h * lax.rsqrt(jnp.mean(h * h, axis=-1, keepdims=True) + EPS) * g_ref[...]
    xn_s[...] = xn.astype(BF16)
    for lo, hi in groups:
        z_s[:, lo:hi] = _dot(xn_s[...], win_ref[:, lo:hi])


def _memory_attention(z_s, q_off, g_off, kt_ref, vm_ref, y_s, y_off, rows=slice(None), norm_on_mxu=False):
    for s in range(M_WIDTH // LANES):
        q = z_s[rows, q_off + LANES * s:q_off + LANES * (s + 1)]
        qn = (q * _pair_rs(q, norm_on_mxu)).astype(BF16)
        acc = None
        for pos in range(2):
            hd = 2 * s + pos
            sc = _dot(qn, kt_ref[hd])
            m = jnp.max(sc, axis=-1, keepdims=True)
            e = jnp.exp2(sc - m)
            l = jnp.sum(e, axis=-1, keepdims=True)
            o = _dot(e.astype(BF16), vm_ref[hd]) * (1.0 / l)
            acc = o if acc is None else acc + o
        gate = _silu(z_s[rows, g_off + LANES * s:g_off + LANES * (s + 1)])
        y_s[rows, y_off + LANES * s:y_off + LANES * (s + 1)] = (acc * gate).astype(BF16)


def _even_tile(t_abs0, h_ref, g_ref, win_ref, wout_ref, lng_ref, lnb_ref, w2_ref, abias_ref, bw_ref,
               bscale_ref, kt_ref, vm_ref, o_ref, xn_s, z_s, y_s, sg_s, xb_s, xb_prev):
    tile = h_ref.shape[0]
    nc = tile // CHUNK
    xb_s[0:POOL_HALO, :] = xb_prev[tile:tile + POOL_HALO, :]

    _project_in(h_ref, g_ref, win_ref, xn_s, z_s, ((E_XB, E_QM), (E_U, E_XB), (E_QM, EVEN_IN)))

    vn = _layer_norm(z_s[:, E_V:E_V + A_WIDTH], lng_ref[...], lnb_ref[...])
    lo = _lo_lanes((CHUNK, LANES))
    row = lax.broadcasted_iota(jnp.int32, (CHUNK, 2 * CHUNK), 0)
    col = lax.broadcasted_iota(jnp.int32, (CHUNK, 2 * CHUNK), 1)
    tril2 = (col & (CHUNK - 1)) <= row
    for s in range(A_WIDTH // LANES):
        tops, bots = [], []
        for c in range(nc):
            blk = vn[CHUNK * c:CHUNK * (c + 1), LANES * s:LANES * (s + 1)]
            tops.append(jnp.where(lo, blk, 0.0).astype(BF16))
            bots.append(jnp.where(lo, 0.0, blk).astype(BF16))
        rhs = jnp.concatenate([jnp.concatenate(tops, axis=1), jnp.concatenate(bots, axis=1)], axis=0)
        w2 = jnp.where(tril2, w2_ref[s], jnp.zeros((), BF16))
        sg = _dot(w2, rhs)
        for c in range(nc):
            sg_s[CHUNK * c:CHUNK * (c + 1), LANES * s:LANES * (s + 1)] = (
                sg[:, LANES * c:LANES * (c + 1)] + abias_ref[s])
    ya = z_s[:, E_U:E_U + A_WIDTH] * sg_s[...] * _silu(z_s[:, E_GA:E_GA + A_WIDTH])
    y_s[:, 0:A_WIDTH] = ya.astype(BF16)

    xb_s[POOL_HALO:POOL_HALO + tile, :] = z_s[:, E_XB:E_XB + B_WIDTH]
    t_abs = t_abs0 + lax.broadcasted_iota(jnp.int32, (tile, 1), 0)
    for g, w in enumerate(POOL_WINDOWS):
        win = xb_s[:, LANES * g:LANES * (g + 1)]
        acc = win
        k = 1
        while k < w:
            acc = acc + pltpu.roll(acc, k, axis=0)
            k *= 2
        cnt = jnp.minimum(t_abs + 1, w).astype(F32)
        pooled = acc[POOL_HALO:, :] * (1.0 / cnt)
        diff = pooled - win[POOL_HALO:, :]
        yb = _dot(diff.astype(BF16), bw_ref[g]) * bscale_ref[:, LANES * g:LANES * (g + 1)]
        yb = yb * _silu(z_s[:, E_GB + LANES * g:E_GB + LANES * (g + 1)])
        y_s[:, A_WIDTH + LANES * g:A_WIDTH + LANES * (g + 1)] = yb.astype(BF16)

    _memory_attention(z_s, E_QM, E_GM, kt_ref, vm_ref, y_s, A_WIDTH + B_WIDTH)

    o_ref[...] = h_ref[...] + _dot(y_s[...], wout_ref[...])


def _sub_tiles(ref):
    n = ref.shape[0]
    return [(ref.at[k], ref.at[(k - 1) % n]) for k in range(n)]


def _even_kernel(h_ref, g_ref, win_ref, wout_ref, lng_ref, lnb_ref, w2_ref, abias_ref, bw_ref,
                 bscale_ref, kt_ref, vm_ref, o_ref, xn_s, z_s, y_s, sg_s, xb_s):
    t = pl.program_id(1)
    nsub, sub = xn_s.shape[0], xn_s.shape[1]

    @pl.when(t == 0)
    def _():
        xb_s[nsub - 1, sub:sub + POOL_HALO, :] = jnp.zeros((POOL_HALO, B_WIDTH), F32)

    for k, (xb, xb_prev) in enumerate(_sub_tiles(xb_s)):
        rows = pl.ds(k * sub, sub)
        _even_tile(t * (nsub * sub) + k * sub, h_ref.at[rows], g_ref, win_ref, wout_ref, lng_ref, lnb_ref,
                   w2_ref, abias_ref, bw_ref, bscale_ref, kt_ref, vm_ref, o_ref.at[rows],
                   xn_s.at[k], z_s.at[k], y_s.at[k], sg_s.at[k], xb, xb_prev)


def _odd_tile(first, sink_row, h_ref, g_ref, win_ref, wout_ref, qg_ref, kg_ref, sink_ref, dw_ref, dwb_ref, dlg_ref,
              dlb_ref, pw_ref, kt_ref, vm_ref, o_ref, xn_s, z_s, y_s, cv_s, hb_s, kb_s, vb_s,
              hb_prev, kb_prev, vb_prev):
    tile = h_ref.shape[0]
    nb = tile // BLOCK
    hb_s[0, 0:CONV_HALO, :] = hb_prev[0, tile:tile + CONV_HALO, :]
    kb_s[:, 0:BLOCK, :] = kb_prev[:, tile:tile + BLOCK, :]
    vb_s[:, 0:BLOCK, :] = vb_prev[:, tile:tile + BLOCK, :]

    halves = [slice(r, r + tile // 2) for r in (0, tile // 2)]
    for rows in halves:
        h = h_ref[rows, :]
        xn_s[rows, :] = (h * lax.rsqrt(jnp.mean(h * h, axis=-1, keepdims=True) + EPS) * g_ref[...]).astype(BF16)
        z_s[rows, O_DA:O_GD] = _dot(xn_s[rows, :], win_ref[:, O_DA:O_GD])
    for c0, c1 in ((O_GD, O_QM), (O_KC, O_GC), (O_QC, O_KC), (O_QM, ODD_IN), (O_GC, O_DA)):
        for rows in halves:
            z_s[rows, c0:c1] = _dot(xn_s[rows, :], win_ref[:, c0:c1])

    lo = _lo_lanes((tile, LANES))
    k = z_s[:, O_KC:O_KC + C_KV_WIDTH]
    kn = k * _pair_rs(k, True) * (kg_ref[...] * qg_ref[...] * (HEAD_DIM ** -0.5 * LOG2E))
    v = z_s[:, O_VC:O_VC + C_KV_WIDTH]
    for buf, x in ((kb_s, kn), (vb_s, v)):
        xsw = pltpu.roll(x, HEAD_DIM, axis=1)
        buf[0, BLOCK:BLOCK + tile, :] = jnp.where(lo, x, 0.0).astype(BF16)
        buf[1, BLOCK:BLOCK + tile, :] = jnp.where(lo, 0.0, x).astype(BF16)
        buf[2, BLOCK:BLOCK + tile, :] = jnp.where(lo, xsw, 0.0).astype(BF16)
        buf[3, BLOCK:BLOCK + tile, :] = jnp.where(lo, 0.0, xsw).astype(BF16)
    variant = ((0, 3), (2, 1))

    qi = lax.broadcasted_iota(jnp.int32, (BLOCK, 2 * BLOCK), 0)
    cj = lax.broadcasted_iota(jnp.int32, (BLOCK, 2 * BLOCK), 1)
    band = (cj - qi >= 1) & (cj - qi <= BLOCK)
    first_key = first * BLOCK
    neg = jnp.finfo(F32).min
    for j in range(nb):
        valid = band & (cj >= first_key) if j == 0 else band
        for s in range(C_Q_WIDTH // LANES):
            q = z_s[BLOCK * j:BLOCK * (j + 1), O_QC + LANES * s:O_QC + LANES * (s + 1)]
            qn = (q * _pair_rs(q, True)).astype(BF16)
            acc = None
            for pos in range(2):
                hd = 2 * s + pos
                var = variant[hd // 4][pos]
                sc = _dot_nt(qn, kb_s[var, BLOCK * j:BLOCK * (j + 2), :])
                sc = jnp.where(valid, sc, neg)
                sink = sink_ref[sink_row, hd] * LOG2E
                m = jnp.maximum(jnp.max(sc, axis=-1, keepdims=True), sink)
                e = jnp.exp2(sc - m)
                l = jnp.sum(e, axis=-1, keepdims=True) + jnp.exp2(sink - m)
                o = _dot(e.astype(BF16), vb_s[var, BLOCK * j:BLOCK * (j + 2), :]) * (1.0 / l)
                acc = o if acc is None else acc + o
            gate = _silu(z_s[BLOCK * j:BLOCK * (j + 1), O_GC + LANES * s:O_GC + LANES * (s + 1)])
            y_s[BLOCK * j:BLOCK * (j + 1), LANES * s:LANES * (s + 1)] = (acc * gate).astype(BF16)

    for rows in halves:
        _memory_attention(z_s, O_QM, O_GM, kt_ref, vm_ref, y_s, C_Q_WIDTH + D_WIDTH, rows, norm_on_mxu=True)

    for rows in halves:
        hb_s[0, CONV_HALO + rows.start:CONV_HALO + rows.stop, :] = (
            z_s[rows, O_DA:O_DA + D_WIDTH] * _sigmoid(z_s[rows, O_DB:O_DB + D_WIDTH]))
    shifted_rows = CONV_HALO + tile - SUBLANES
    for s in range(D_WIDTH // LANES):
        x = hb_s[0, :, LANES * s:LANES * (s + 1)]
        for p in range(1, SUBLANES):
            hb_s[p, 0:shifted_rows, LANES * s:LANES * (s + 1)] = pltpu.roll(
                x, CONV_HALO + tile - p, axis=0)[0:shifted_rows, :]
    base = CONV_HALO - (CONV_WIDTH - 1)
    for s in range(D_WIDTH // LANES):
        for j in range(nb):
            acc = jnp.broadcast_to(dwb_ref[:, LANES * s:LANES * (s + 1)], (BLOCK, LANES))
            for tap in range(CONV_WIDTH):
                p, r = (base + tap) % SUBLANES, (base + tap) // SUBLANES * SUBLANES
                x = hb_s[p, BLOCK * j + r:BLOCK * (j + 1) + r, LANES * s:LANES * (s + 1)]
                acc = acc + x * dw_ref[tap:tap + 1, LANES * s:LANES * (s + 1)]
            cv_s[BLOCK * j:BLOCK * (j + 1), LANES * s:LANES * (s + 1)] = acc
    for rows in halves:
        hd_ = _silu(_layer_norm(cv_s[rows, :], dlg_ref[...], dlb_ref[...]))
        yd = _dot(hd_.astype(BF16), pw_ref[...]) * _silu(z_s[rows, O_GD:O_GD + D_WIDTH])
        y_s[rows, C_Q_WIDTH:C_Q_WIDTH + D_WIDTH] = yd.astype(BF16)

    for rows in halves:
        o_ref[rows, :] = h_ref[rows, :] + _dot(y_s[rows, :], wout_ref[...])


def _odd_kernel(sink_row, h_ref, g_ref, win_ref, wout_ref, qg_ref, kg_ref, sink_ref, dw_ref, dwb_ref, dlg_ref,
                dlb_ref, pw_ref, kt_ref, vm_ref, o_ref, xn_s, z_s, y_s, cv_s, hb_s, kb_s, vb_s):
    t = pl.program_id(1)
    nsub, sub = xn_s.shape[0], xn_s.shape[1]

    @pl.when(t == 0)
    def _():
        hb_s[nsub - 1, 0, sub:sub + CONV_HALO, :] = jnp.zeros((CONV_HALO, D_WIDTH), F32)
        kb_s[nsub - 1, :, sub:sub + BLOCK, :] = jnp.zeros((4, BLOCK, LANES), BF16)
        vb_s[nsub - 1, :, sub:sub + BLOCK, :] = jnp.zeros((4, BLOCK, LANES), BF16)

    halos = zip(_sub_tiles(hb_s), _sub_tiles(kb_s), _sub_tiles(vb_s))
    for k, ((hb, hb_prev), (kb, kb_prev), (vb, vb_prev)) in enumerate(halos):
        rows = pl.ds(k * sub, sub)
        first = (t == 0).astype(jnp.int32) if k == 0 else 0
        _odd_tile(first, sink_row, h_ref.at[rows], g_ref, win_ref, wout_ref, qg_ref, kg_ref, sink_ref, dw_ref,
                  dwb_ref, dlg_ref, dlb_ref, pw_ref, kt_ref, vm_ref, o_ref.at[rows], xn_s.at[k], z_s.at[k],
                  y_s.at[k], cv_s.at[k], hb, kb, vb, hb_prev, kb_prev, vb_prev)


def _mem_prep_kernel(mem_ref, g_ref, w_ref, qn_ref, kn_ref, kt_ref, vm_ref):
    m = mem_ref[...]
    mn = (m * lax.rsqrt(jnp.mean(m * m, axis=-1, keepdims=True) + EPS) * g_ref[...]).astype(BF16)
    zeros = jnp.zeros((HEAD_DIM, N_MEM), F32)
    lo = _lo_lanes((N_MEM, LANES))
    for layer in range(w_ref.shape[0]):
        kv = _dot(mn, w_ref[layer])
        kt = kv[:, 0:M_WIDTH].T
        gain = kn_ref[layer] * qn_ref[layer] * (HEAD_DIM ** -0.5 * LOG2E)
        for hd in range(M_HEADS):
            kh = kt[HEAD_DIM * hd:HEAD_DIM * (hd + 1), :]
            kh = kh * lax.rsqrt(jnp.mean(kh * kh, axis=0, keepdims=True) + EPS) * gain
            parts = [kh, zeros] if hd % 2 == 0 else [zeros, kh]
            kt_ref[layer, hd] = jnp.concatenate(parts, axis=0).astype(BF16)
            s = hd // 2
            v = kv[:, M_WIDTH + LANES * s:M_WIDTH + LANES * (s + 1)]
            vm_ref[layer, hd] = (jnp.where(lo, v, 0.0) if hd % 2 == 0 else jnp.where(lo, 0.0, v)).astype(BF16)


def _mem_prep(mem, mem_norm_g, w_mem_kv, m_qnorm, m_knorm):
    depth, batch = w_mem_kv.shape[0], mem.shape[0]
    whole = lambda shape: pl.BlockSpec(shape, lambda b: (0,) * len(shape))
    return pl.pallas_call(
        _mem_prep_kernel,
        grid=(batch,),
        in_specs=[
            pl.BlockSpec((None, N_MEM, D_MODEL), lambda b: (b, 0, 0)),
            whole((1, D_MODEL)),
            whole((depth, D_MODEL, 2 * M_WIDTH)),
            whole((depth, HEAD_DIM, 1)),
            whole((depth, HEAD_DIM, 1)),
        ],
        out_specs=[
            pl.BlockSpec((depth, None, M_HEADS, LANES, N_MEM), lambda b: (0, b, 0, 0, 0)),
            pl.BlockSpec((depth, None, M_HEADS, N_MEM, LANES), lambda b: (0, b, 0, 0, 0)),
        ],
        out_shape=[
            jax.ShapeDtypeStruct((depth, batch, M_HEADS, LANES, N_MEM), BF16),
            jax.ShapeDtypeStruct((depth, batch, M_HEADS, N_MEM, LANES), BF16),
        ],
        compiler_params=pltpu.CompilerParams(dimension_semantics=("arbitrary",)),
        name="mem_prep",
    )(mem, mem_norm_g.reshape(1, D_MODEL), w_mem_kv.astype(BF16),
      m_qnorm.reshape(depth, HEAD_DIM, 1), m_knorm.reshape(depth, HEAD_DIM, 1))


def _layer_of(stacked, index):
    tail = stacked.shape[1:]
    return pl.BlockSpec((None,) + tail, lambda b, t: (index,) + (0,) * len(tail), pipeline_mode=pl.Buffered(1))


def _rows(x):
    return x.reshape(x.shape[0], 1, x.shape[1])


def _layer_call(body, name, h, in_width, tile, nsub, scratch, operands, layer, kt, vm):
    batch, seq, _ = h.shape
    sub = tile // nsub
    h_spec = pl.BlockSpec((None, tile, D_MODEL), lambda b, t: (b, t, 0))
    mem_specs = [
        pl.BlockSpec((None, None, M_HEADS, LANES, N_MEM), lambda b, t: (layer, b, 0, 0, 0)),
        pl.BlockSpec((None, None, M_HEADS, N_MEM, LANES), lambda b, t: (layer, b, 0, 0, 0)),
    ]
    return pl.pallas_call(
        body,
        grid=(batch, seq // tile),
        in_specs=[h_spec] + [spec for _, spec in operands] + mem_specs,
        out_specs=h_spec,
        out_shape=jax.ShapeDtypeStruct(h.shape, h.dtype),
        scratch_shapes=[
            pltpu.VMEM((nsub, sub, D_MODEL), BF16),
            pltpu.VMEM((nsub, sub, in_width), F32),
            pltpu.VMEM((nsub, sub, OUT_WIDTH), BF16),
        ] + scratch,
        compiler_params=pltpu.CompilerParams(
            dimension_semantics=("arbitrary", "arbitrary"), vmem_limit_bytes=VMEM_LIMIT),
        name=name,
    )(h, *[x for x, _ in operands], kt, vm)


def kernel(x, mem, norm_g, mem_norm_g, w_mem_kv, m_qnorm, m_knorm, w_in_even, w_out_even, a_ln_g, a_ln_b, a_ws, a_bs, b_w, b_scale, w_in_odd, w_out_odd, c_qnorm, c_knorm, c_sink, d_dw, d_dw_b, d_ln_g, d_ln_b, d_pw):
    kt, vm = _mem_prep(mem, mem_norm_g, w_mem_kv, m_qnorm, m_knorm)

    n_even, heads = a_ws.shape[0], a_ws.shape[1]
    norm_rows = _rows(norm_g)
    even = dict(
        w_in=w_in_even.astype(BF16), w_out=w_out_even.astype(BF16), ln_g=_rows(a_ln_g), ln_b=_rows(a_ln_b),
        w2=a_ws.reshape(n_even, heads // 2, 2, CHUNK, CHUNK).transpose(0, 1, 3, 2, 4)
               .reshape(n_even, heads // 2, CHUNK, 2 * CHUNK).astype(BF16),
        abias=jnp.repeat(a_bs.reshape(n_even, heads // 2, 2, CHUNK).transpose(0, 1, 3, 2), HEAD_DIM, axis=3),
        bw=b_w.astype(BF16), bscale=_rows(b_scale))
    odd = dict(
        w_in=w_in_odd.astype(BF16), w_out=w_out_odd.astype(BF16),
        qg=_rows(jnp.tile(c_qnorm, (1, 2))), kg=_rows(jnp.tile(c_knorm, (1, 2))),
        dw=d_dw, dwb=_rows(d_dw_b), ln_g=_rows(d_ln_g), ln_b=_rows(d_ln_b), pw=d_pw.astype(BF16))
    even_scratch = [pltpu.VMEM((EVEN_SUB_TILES, EVEN_TILE // EVEN_SUB_TILES, A_WIDTH), F32),
                    pltpu.VMEM((EVEN_SUB_TILES, POOL_HALO + EVEN_TILE // EVEN_SUB_TILES, B_WIDTH), F32)]
    odd_sub = TILE // ODD_SUB_TILES
    odd_scratch = [pltpu.VMEM((ODD_SUB_TILES, odd_sub, D_WIDTH), F32),
                   pltpu.VMEM((ODD_SUB_TILES, SUBLANES, CONV_HALO + odd_sub, D_WIDTH), F32),
                   pltpu.VMEM((ODD_SUB_TILES, 4, BLOCK + odd_sub, LANES), BF16),
                   pltpu.VMEM((ODD_SUB_TILES, 4, BLOCK + odd_sub, LANES), BF16)]

    h = x
    for layer in range(w_mem_kv.shape[0]):
        i = layer // 2
        norm = (norm_rows, _layer_of(norm_rows, layer))
        if layer % 2 == 0:
            names = ("w_in", "w_out", "ln_g", "ln_b", "w2", "abias", "bw", "bscale")
            operands = [norm] + [(even[n], _layer_of(even[n], i)) for n in names]
            h = _layer_call(_even_kernel, "even_layer", h, EVEN_IN, EVEN_TILE, EVEN_SUB_TILES, even_scratch, operands,
                            layer, kt, vm)
        else:
            operands = [norm] + [(odd[n], _layer_of(odd[n], i)) for n in ("w_in", "w_out", "qg", "kg")]
            operands.append((c_sink, pl.BlockSpec(memory_space=pltpu.SMEM)))
            operands += [(odd[n], _layer_of(odd[n], i)) for n in ("dw", "dwb", "ln_g", "ln_b", "pw")]
            h = _layer_call(functools.partial(_odd_kernel, i), "odd_layer", h, ODD_IN, TILE, ODD_SUB_TILES,
                            odd_scratch, operands, layer, kt, vm)
    return h
```
